```python
import jax, jax.numpy as jnp
from jax import lax
import numpy as np

D_MODEL = 1024
BATCH = 2
SEQ = 16384
DEPTH = 1

CHUNK = 64
Q_BLOCK = 2 * CHUNK
ATTN_WIDTH = D_MODEL // 2
CONV_WIDTH = D_MODEL - ATTN_WIDTH
HEAD_DIM = 64
N_HEADS = ATTN_WIDTH // HEAD_DIM
CONV_KERNEL = 31
D_FF = 4 * D_MODEL
IN_WIDTH = 3 * ATTN_WIDTH + N_HEADS + 2 * CONV_WIDTH
EPS = 1e-6

kernel_name = "hybrid_fox_conformer_conv_adaln_block"


def _rmsnorm(x, g):
    xf = x.astype(jnp.float32)
    y = xf * lax.rsqrt(jnp.mean(xf * xf, axis=-1, keepdims=True) + EPS)
    return (y * g.astype(jnp.float32)).astype(x.dtype)


def _layernorm(x, g, b):
    xf = x.astype(jnp.float32)
    mu = jnp.mean(xf, axis=-1, keepdims=True)
    var = jnp.mean(jnp.square(xf - mu), axis=-1, keepdims=True)
    y = (xf - mu) * lax.rsqrt(var + EPS)
    return (y * g.astype(jnp.float32) + b.astype(jnp.float32)).astype(x.dtype)


def _fox_attention(q, k, v, log_f):
    B, S, H, Dh = q.shape
    nb = S // Q_BLOCK
    F = jnp.cumsum(log_f, axis=1).transpose(0, 2, 1)
    qh = q.transpose(0, 2, 1, 3)
    kh = k.transpose(0, 2, 1, 3)
    vh = v.transpose(0, 2, 1, 3)
    q_blocks = qh.reshape(B, H, nb, Q_BLOCK, Dh).transpose(2, 0, 1, 3, 4)
    F_blocks = F.reshape(B, H, nb, Q_BLOCK).transpose(2, 0, 1, 3)
    k_pos = jnp.arange(S)
    scale = HEAD_DIM ** -0.5

    def block(args):
        qb, Fb, i = args
        logits = jnp.einsum('bhqd,bhkd->bhqk', qb, kh).astype(jnp.float32) * scale
        logits = logits + Fb[..., None] - F[:, :, None, :]
        q_pos = i * Q_BLOCK + jnp.arange(Q_BLOCK)
        mask = k_pos[None, :] <= q_pos[:, None]
        logits = jnp.where(mask[None, None], logits, -jnp.inf)
        p = jax.nn.softmax(logits, axis=-1)
        return jnp.einsum('bhqk,bhkd->bhqd', p.astype(vh.dtype), vh)

    out = lax.map(block, (q_blocks, F_blocks, jnp.arange(nb)))
    return out.transpose(1, 0, 3, 2, 4).reshape(B, S, H * Dh)


def _causal_depthwise_conv(u, w, b):
    K, C = w.shape
    u_pad = jnp.pad(u, ((0, 0), (K - 1, 0), (0, 0)))
    y = lax.conv_general_dilated(u_pad, w[:, None, :], window_strides=(1,), padding='VALID',
                                 dimension_numbers=('NWC', 'WIO', 'NWC'),
                                 feature_group_count=C)
    return y + b


def setup_inputs(seed: int = 0) -> dict:
    key = jax.random.key(seed)
    ks = jax.random.split(key, 20)
    f32 = jnp.float32
    L, D, A, Cw, H = DEPTH, D_MODEL, ATTN_WIDTH, CONV_WIDTH, N_HEADS
    nrm = lambda k, shape, s: jax.random.normal(k, shape, f32) * s
    return {
        "x": jax.random.normal(ks[0], (BATCH, SEQ, D), f32),
        "c": jax.random.normal(ks[1], (BATCH, D), f32),
        "w_ada": nrm(ks[2], (L, D, 6 * D), 0.5 * D ** -0.5),
        "b_ada": nrm(ks[3], (L, 6 * D), 0.02),
        "norm1_g": 1.0 + nrm(ks[4], (L, D), 0.02),
        "w_in": nrm(ks[5], (L, D, IN_WIDTH), D ** -0.5),
        "q_norm_g": 1.0 + nrm(ks[6], (L, HEAD_DIM), 0.02),
        "k_norm_g": 1.0 + nrm(ks[7], (L, HEAD_DIM), 0.02),
        "b_f": jax.random.uniform(ks[8], (L, H), f32, minval=1.0, maxval=6.0),
        "conv_w": nrm(ks[9], (L, CONV_KERNEL, Cw), CONV_KERNEL ** -0.5),
        "conv_b": nrm(ks[10], (L, Cw), 0.02),
        "conv_ln_g": 1.0 + nrm(ks[11], (L, Cw), 0.02),
        "conv_ln_b": nrm(ks[12], (L, Cw), 0.02),
        "beta_attn": 1.0 + nrm(ks[13], (L, A), 0.02),
        "beta_conv": 1.0 + nrm(ks[14], (L, Cw), 0.02),
        "w_out": nrm(ks[15], (L, D, D), D ** -0.5),
        "norm2_g": 1.0 + nrm(ks[16], (L, D), 0.02),
        "w_ff1": nrm(ks[17], (L, D, D_FF), D ** -0.5),
        "w_ff2": nrm(ks[18], (L, D_FF, D), D_FF ** -0.5),
    }


def reference(x, c, w_ada, b_ada, norm1_g, w_in, q_norm_g, k_norm_g, b_f, conv_w, conv_b,
              conv_ln_g, conv_ln_b, beta_attn, beta_conv, w_out, norm2_g, w_ff1, w_ff2):
    B, S, D = x.shape
    A, H = ATTN_WIDTH, N_HEADS
    split_pts = [A, 2 * A, 3 * A, 3 * A + H]
    for l in range(DEPTH):
        mod = jax.nn.silu(c) @ w_ada[l] + b_ada[l]
        sh1, sc1, g1, sh2, sc2, g2 = [m[:, None, :] for m in jnp.split(mod, 6, axis=-1)]

        h = _rmsnorm(x, norm1_g[l]) * (1 + sc1) + sh1
        z = h @ w_in[l]
        q, k, v, fg, conv_in = jnp.split(z, split_pts, axis=-1)

        q = _rmsnorm(q.reshape(B, S, H, HEAD_DIM), q_norm_g[l])
        k = _rmsnorm(k.reshape(B, S, H, HEAD_DIM), k_norm_g[l])
        v = v.reshape(B, S, H, HEAD_DIM)
        log_f = jax.nn.log_sigmoid(fg.astype(jnp.float32) + b_f[l].astype(jnp.float32))
        attn = _fox_attention(q, k, v, log_f)

        a_lin, a_gate = jnp.split(conv_in, 2, axis=-1)
        u = a_lin * jax.nn.sigmoid(a_gate)
        u = _causal_depthwise_conv(u, conv_w[l], conv_b[l])
        u = jax.nn.silu(_layernorm(u, conv_ln_g[l], conv_ln_b[l]))

        merged = jnp.concatenate([_rmsnorm(attn, beta_attn[l]), _rmsnorm(u, beta_conv[l])], axis=-1)
        x = x + g1 * (merged @ w_out[l])

        h = _rmsnorm(x, norm2_g[l]) * (1 + sc2) + sh2
        x = x + g2 * (jnp.square(jax.nn.relu(h @ w_ff1[l])) @ w_ff2[l])
    return x
```

```python
import functools

import jax
import jax.numpy as jnp
from jax import lax
from jax.experimental import pallas as pl
from jax.experimental.pallas import tpu as pltpu

F32 = jnp.float32
BF16 = jnp.bfloat16

HEAD_DIM = 64
N_HEADS = 8
CONV_KERNEL = 31
EPS = 1e-6

LANES = 128
SUBLANES = 8
BF16_ROWS = 16
AUG = 128
V_ROWS = HEAD_DIM + BF16_ROWS
HALO = 32
NEG = -1e30

TM_PROJ = 512
TQ = 256
TK = 256
TM_OUT = 512
FF_CHUNK = 1024
VMEM_LIMIT = 56 * 1024 * 1024


def _sigmoid(x):
    return 1.0 / (1.0 + jnp.exp(-x))


def _split3(x):
    hi = x.astype(BF16).astype(F32)
    r = x - hi
    mid = r.astype(BF16).astype(F32)
    lo = (r - mid).astype(BF16).astype(F32)
    return hi, mid, lo


def _ada_kernel(c_ref, w_ref, b_ref, o_ref):
    c = c_ref[...]
    a = c * _sigmoid(c)
    o_ref[...] = jnp.dot(a, w_ref[...], preferred_element_type=F32,
                         precision=lax.Precision.HIGHEST) + b_ref[...]


def _ada(c, w, b):
    bsz, d = c.shape
    n = w.shape[1]
    rows = -(-bsz // SUBLANES) * SUBLANES
    cp = jnp.zeros((rows, d), F32).at[:bsz].set(c)
    tn = 1024
    out = pl.pallas_call(
        _ada_kernel,
        grid=(n // tn,),
        in_specs=[pl.BlockSpec((rows, d), lambda j: (0, 0)),
                  pl.BlockSpec((d, tn), lambda j: (0, j)),
                  pl.BlockSpec((1, tn), lambda j: (0, j))],
        out_specs=pl.BlockSpec((rows, tn), lambda j: (0, j)),
        out_shape=jax.ShapeDtypeStruct((rows, n), F32),
        compiler_params=pltpu.CompilerParams(dimension_semantics=("arbitrary",)),
    )(cp, w, b.reshape(1, n))
    return out[:bsz]


def _proj_kernel(x_ref, mod_ref, g_ref, wT_ref, wc_ref, qg_ref, bf_ref, cw_ref, cb_ref,
                 lng_ref, lnb_ref, bc_ref,
                 qT_ref, k_ref, vT_ref, un_ref,
                 ubuf, carry, tri, *, tk):
    t = pl.program_id(1)
    tm = x_ref.shape[1]
    a_w = N_HEADS * HEAD_DIM
    c_w = un_ref.shape[2]

    @pl.when(t == 0)
    def _():
        ubuf[0:HALO, :] = jnp.zeros((HALO, c_w), F32)
        carry[...] = jnp.zeros_like(carry)
        r = lax.broadcasted_iota(jnp.int32, (tm, tm), 0)
        c = lax.broadcasted_iota(jnp.int32, (tm, tm), 1)
        tri[...] = jnp.where(r <= c, 1.0, 0.0).astype(BF16)

    x = x_ref[0]
    ms = jnp.mean(x * x, axis=-1, keepdims=True)
    h = x * lax.rsqrt(ms + EPS) * g_ref[...]
    h = h * (1.0 + mod_ref[0, 1:2, :]) + mod_ref[0, 0:1, :]
    hb = h.astype(BF16)

    zT = lax.dot_general(wT_ref[...], hb, (((1,), (1,)), ((), ())), preferred_element_type=F32)

    f = zT[3 * a_w:3 * a_w + N_HEADS] + jnp.tile(bf_ref[...], (1, tm // LANES))
    logf = jnp.minimum(f, 0.0) - jnp.log1p(jnp.exp(-jnp.abs(f)))
    parts = jnp.concatenate(list(_split3(logf)) + [jnp.zeros((SUBLANES, tm), F32)], axis=0).astype(BF16)
    within = jnp.dot(parts, tri[...], preferred_element_type=F32)
    total = jnp.dot(parts, jnp.ones((tm, LANES), BF16), preferred_element_type=F32)
    fold = lambda a: a[0:8] + a[8:16] + a[16:24]
    FT = fold(within) + jnp.tile(carry[...], (1, tm // LANES))
    carry[...] = carry[...] + fold(total)
    Fh, Fm, Fl = _split3(FT)

    ri = lax.broadcasted_iota(jnp.int32, (SUBLANES, tm), 0)
    zpad_qk = jnp.zeros((AUG - HEAD_DIM - SUBLANES, tm), F32)
    xv = jnp.where(ri == 0, 1.0, 0.0)
    zpad_v = jnp.zeros((V_ROWS - HEAD_DIM - SUBLANES, tm), F32)
    qg = jnp.tile(qg_ref[...], (1, tm // LANES))
    for hd in range(N_HEADS):
        lo, hi = hd * HEAD_DIM, (hd + 1) * HEAD_DIM
        fh, fm, fl = Fh[hd:hd + 1], Fm[hd:hd + 1], Fl[hd:hd + 1]
        qh = zT[lo:hi]
        qn = qh * lax.rsqrt(jnp.mean(qh * qh, axis=0, keepdims=True) + EPS) * qg
        xq = jnp.where(ri == 0, fh, jnp.where(ri == 1, fm, jnp.where(ri == 2, fl,
                       jnp.where(ri < 6, 1.0, 0.0))))
        qT_ref[0, hd] = jnp.concatenate([qn, xq, zpad_qk], axis=0).astype(BF16)
        kh = zT[a_w + lo:a_w + hi]
        kn = kh * lax.rsqrt(jnp.mean(kh * kh, axis=0, keepdims=True) + EPS)
        xk = jnp.where(ri < 3, 1.0, jnp.where(ri == 3, -fh, jnp.where(ri == 4, -fm,
                       jnp.where(ri == 5, -fl, 0.0))))
        k_ref[0, hd] = jnp.concatenate([kn, xk, zpad_qk], axis=0).T.astype(BF16)
        vaug = jnp.concatenate([zT[2 * a_w + lo:2 * a_w + hi], xv, zpad_v], axis=0).astype(BF16)
        for cidx in range(tm // tk):
            vT_ref[0, hd, cidx] = vaug[:, cidx * tk:(cidx + 1) * tk]

    zc = jnp.dot(hb, wc_ref[...], preferred_element_type=F32)
    ubuf[HALO:HALO + tm, :] = zc[:, :c_w] * _sigmoid(zc[:, c_w:])
    y = jnp.broadcast_to(cb_ref[...], (tm, c_w))
    for j in range(CONV_KERNEL):
        y = y + ubuf[pl.ds(j + HALO - (CONV_KERNEL - 1), tm), :] * cw_ref[j:j + 1, :]
    ubuf[0:HALO, :] = ubuf[tm:tm + HALO, :]
    mu = jnp.mean(y, axis=-1, keepdims=True)
    yc = y - mu
    var = jnp.mean(yc * yc, axis=-1, keepdims=True)
    y = yc * lax.rsqrt(var + EPS) * lng_ref[...] + lnb_ref[...]
    y = y * _sigmoid(y)
    y = y * lax.rsqrt(jnp.mean(y * y, axis=-1, keepdims=True) + EPS) * bc_ref[...]
    un_ref[0] = y.astype(BF16)


def _proj(x, mod3, norm_g, wT, wc, qg, bfb, cw, cb, lng, lnb, bc, *, tm, tk):
    bsz, s, d = x.shape
    c_w = cw.shape[1]
    nt = s // tm
    const = lambda shape: pl.BlockSpec(shape, lambda b, t: (0,) * len(shape))
    return pl.pallas_call(
        functools.partial(_proj_kernel, tk=tk),
        grid=(bsz, nt),
        in_specs=[
            pl.BlockSpec((1, tm, d), lambda b, t: (b, t, 0)),
            pl.BlockSpec((1, 6, d), lambda b, t: (b, 0, 0)),
            const((1, d)),
            const(wT.shape),
            const(wc.shape),
            const(qg.shape),
            const(bfb.shape),
            const(cw.shape),
            const((1, c_w)),
            const((1, c_w)),
            const((1, c_w)),
            const((1, c_w)),
        ],
        out_specs=[
            pl.BlockSpec((1, N_HEADS, AUG, tm), lambda b, t: (b, 0, 0, t)),
            pl.BlockSpec((1, N_HEADS, tm, AUG), lambda b, t: (b, 0, t, 0)),
            pl.BlockSpec((1, N_HEADS, tm // tk, V_ROWS, tk), lambda b, t: (b, 0, t, 0, 0)),
            pl.BlockSpec((1, tm, c_w), lambda b, t: (b, t, 0)),
        ],
        out_shape=[
            jax.ShapeDtypeStruct((bsz, N_HEADS, AUG, s), BF16),
            jax.ShapeDtypeStruct((bsz, N_HEADS, s, AUG), BF16),
            jax.ShapeDtypeStruct((bsz, N_HEADS, s // tk, V_ROWS, tk), BF16),
            jax.ShapeDtypeStruct((bsz, s, c_w), BF16),
        ],
        scratch_shapes=[
            pltpu.VMEM((HALO + tm, c_w), F32),
            pltpu.VMEM((N_HEADS, LANES), F32),
            pltpu.VMEM((tm, tm), BF16),
        ],
        compiler_params=pltpu.CompilerParams(
            dimension_semantics=("arbitrary", "arbitrary"), vmem_limit_bytes=VMEM_LIMIT),
    )(x, mod3, norm_g, wT, wc, qg, bfb, cw, cb, lng, lnb, bc)


def _attn_kernel(qT_ref, k_ref, vT_ref, o_ref, *, tq, tk):
    i = pl.program_id(2)
    qT = qT_ref[0, 0]

    def tile(j, m, acc, masked):
        kt = k_ref[0, 0, pl.ds(pl.multiple_of(j * tk, tk), tk), :]
        s = jnp.dot(kt, qT, preferred_element_type=F32)
        if masked:
            key = j * tk + lax.broadcasted_iota(jnp.int32, (tk, tq), 0)
            qry = i * tq + lax.broadcasted_iota(jnp.int32, (tk, tq), 1)
            s = jnp.where(key <= qry, s, NEG)
        m_new = jnp.maximum(m, jnp.max(s, axis=0, keepdims=True))
        alpha = jnp.exp(m - m_new)
        p = jnp.exp(s - m_new).astype(BF16)
        acc = acc * alpha + jnp.dot(vT_ref[0, 0, j], p, preferred_element_type=F32)
        return m_new, acc

    m0 = jnp.full((1, tq), NEG, F32)
    acc0 = jnp.zeros((V_ROWS, tq), F32)
    n_full = i * (tq // tk)
    m, acc = lax.fori_loop(0, n_full, lambda j, c: tile(j, c[0], c[1], False), (m0, acc0))
    for dgn in range(tq // tk):
        m, acc = tile(n_full + dgn, m, acc, True)
    o_ref[0] = acc[0:HEAD_DIM] * (1.0 / acc[HEAD_DIM:HEAD_DIM + 1])


def _attention(qT, k, vT, *, tq, tk):
    bsz, nh, _, s = qT.shape
    return pl.pallas_call(
        functools.partial(_attn_kernel, tq=tq, tk=tk),
        grid=(bsz, nh, s // tq),
        in_specs=[
            pl.BlockSpec((1, 1, AUG, tq), lambda b, h, i: (b, h, 0, i)),
            pl.BlockSpec((1, 1, s, AUG), lambda b, h, i: (b, h, 0, 0)),
            pl.BlockSpec((1, 1, s // tk, V_ROWS, tk), lambda b, h, i: (b, h, 0, 0, 0)),
        ],
        out_specs=pl.BlockSpec((1, HEAD_DIM, tq), lambda b, h, i: (b, h, i)),
        out_shape=jax.ShapeDtypeStruct((bsz, nh * HEAD_DIM, s), F32),
        compiler_params=pltpu.CompilerParams(
            dimension_semantics=("arbitrary", "arbitrary", "arbitrary"),
            vmem_limit_bytes=VMEM_LIMIT),
    )(qT, k, vT)


def _out_kernel(x_ref, aT_ref, un_ref, mod_ref, ba_ref, g2_ref, wo_ref, w1_ref, w2_ref, o_ref):
    a_w = aT_ref.shape[1]
    d_ff = w1_ref.shape[1]
    aT = aT_ref[0]
    inv = lax.rsqrt(jnp.mean(aT * aT, axis=0, keepdims=True) + EPS)
    an = ((aT * inv).T * ba_ref[...]).astype(BF16)
    o = jnp.dot(an, wo_ref[0:a_w, :], preferred_element_type=F32)
    o = o + jnp.dot(un_ref[0], wo_ref[a_w:, :], preferred_element_type=F32)
    x1 = x_ref[0] + mod_ref[0, 2:3, :] * o
    h = x1 * lax.rsqrt(jnp.mean(x1 * x1, axis=-1, keepdims=True) + EPS) * g2_ref[...]
    hb = (h * (1.0 + mod_ref[0, 4:5, :]) + mod_ref[0, 3:4, :]).astype(BF16)
    y = jnp.zeros_like(x1)
    for c in range(d_ff // FF_CHUNK):
        a = jnp.dot(hb, w1_ref[:, c * FF_CHUNK:(c + 1) * FF_CHUNK], preferred_element_type=F32)
        a = jnp.maximum(a, 0.0)
        y = y + jnp.dot((a * a).astype(BF16), w2_ref[c * FF_CHUNK:(c + 1) * FF_CHUNK, :],
                        preferred_element_type=F32)
    o_ref[0] = x1 + mod_ref[0, 5:6, :] * y


def _out(x, attnT, un, mod3, beta_a, norm_g, wo, w1, w2, *, tm):
    bsz, s, d = x.shape
    a_w = attnT.shape[1]
    c_w = un.shape[2]
    resident = lambda shape: pl.BlockSpec(shape, lambda b, t: (0,) * len(shape),
                                          pipeline_mode=pl.Buffered(1))
    return pl.pallas_call(
        _out_kernel,
        grid=(bsz, s // tm),
        in_specs=[
            pl.BlockSpec((1, tm, d), lambda b, t: (b, t, 0)),
            pl.BlockSpec((1, a_w, tm), lambda b, t: (b, 0, t)),
            pl.BlockSpec((1, tm, c_w), lambda b, t: (b, t, 0)),
            pl.BlockSpec((1, 6, d), lambda b, t: (b, 0, 0)),
            resident((1, a_w)),
            resident((1, d)),
            resident(wo.shape),
            resident(w1.shape),
            resident(w2.shape),
        ],
        out_specs=pl.BlockSpec((1, tm, d), lambda b, t: (b, t, 0)),
        out_shape=jax.ShapeDtypeStruct((bsz, s, d), F32),
        compiler_params=pltpu.CompilerParams(
            dimension_semantics=("arbitrary", "arbitrary"), vmem_limit_bytes=VMEM_LIMIT),
    )(x, attnT, un, mod3, beta_a, norm_g, wo, w1, w2)


def kernel(x, c, w_ada, b_ada, norm1_g, w_in, q_norm_g, k_norm_g, b_f, conv_w, conv_b,
           conv_ln_g, conv_ln_b, beta_attn, beta_conv, w_out, norm2_g, w_ff1, w_ff2):
    bsz, s, d = x.shape
    a_w = N_HEADS * HEAD_DIM
    c_w = conv_w.shape[2]
    assert s % TM_PROJ == 0 and s % TQ == 0 and s % TM_OUT == 0 and TQ % TK == 0 and TM_PROJ % TK == 0
    assert w_in.shape[2] == 3 * a_w + N_HEADS + 2 * c_w
    for l in range(w_ada.shape[0]):
        mod3 = _ada(c, w_ada[l], b_ada[l]).reshape(bsz, 6, d)

        w = w_in[l]
        wf = jnp.zeros((d, BF16_ROWS), F32).at[:, :N_HEADS].set(w[:, 3 * a_w:3 * a_w + N_HEADS])
        wT = jnp.concatenate([w[:, :3 * a_w], wf], axis=1).T.astype(BF16)
        wc = w[:, 3 * a_w + N_HEADS:].astype(BF16)
        qg = jnp.broadcast_to((q_norm_g[l] * k_norm_g[l] * HEAD_DIM ** -0.5)[:, None],
                              (HEAD_DIM, LANES))
        bfb = jnp.broadcast_to(b_f[l][:, None], (N_HEADS, LANES))
        row = lambda v: v.reshape(1, -1)

        qT, k, vT, un = _proj(x, mod3, row(norm1_g[l]), wT, wc, qg, bfb, conv_w[l],
                              row(conv_b[l]), row(conv_ln_g[l]), row(conv_ln_b[l]),
                              row(beta_conv[l]), tm=TM_PROJ, tk=TK)
        attnT = _attention(qT, k, vT, tq=TQ, tk=TK)
        x = _out(x, attnT, un, mod3, row(beta_attn[l]), row(norm2_g[l]),
                 w_out[l].astype(BF16), w_ff1[l].astype(BF16), w_ff2[l].astype(BF16), tm=TM_OUT)
    return x
```

```python
import functools

import jax
import jax.numpy as jnp
from jax import lax
from jax.experimental import pallas as pl
from jax.experimental.pallas import tpu as pltpu

F32 = jnp.float32
BF16 = jnp.bfloat16

HEAD_DIM = 64
N_HEADS = 8
CONV_KERNEL = 31
EPS = 1e-6

LANES = 128
SUBLANES = 8
BF16_ROWS = 16
AUG = 128
V_ROWS = HEAD_DIM + BF16_ROWS
HALO = 32
NEG = -1e30
LOG2E = 1.4426950408889634
BOUND_SLACK = 1.0 + 2.0 ** -6
F32_MIN_EXP2 = -126.0

TM_PROJ = 512
TQ = 512
TK = 512
HEAD_GROUP = 2
TM_OUT = 512
FF_CHUNK = 1024
VMEM_LIMIT = 56 * 1024 * 1024


def _sigmoid(x):
    return 1.0 / (1.0 + jnp.exp(-x))


def _split3(x):
    hi = x.astype(BF16).astype(F32)
    r = x - hi
    mid = r.astype(BF16).astype(F32)
    lo = (r - mid).astype(BF16).astype(F32)
    return hi, mid, lo


def _ada_kernel(c_ref, w_ref, b_ref, o_ref):
    c = c_ref[...]
    a = c * _sigmoid(c)
    o_ref[...] = jnp.dot(a, w_ref[...], preferred_element_type=F32,
                         precision=lax.Precision.HIGHEST) + b_ref[...]


def _ada(c, w, b):
    bsz, d = c.shape
    n = w.shape[1]
    rows = -(-bsz // SUBLANES) * SUBLANES
    cp = jnp.zeros((rows, d), F32).at[:bsz].set(c)
    tn = 1024
    out = pl.pallas_call(
        _ada_kernel,
        grid=(n // tn,),
        in_specs=[pl.BlockSpec((rows, d), lambda j: (0, 0)),
                  pl.BlockSpec((d, tn), lambda j: (0, j)),
                  pl.BlockSpec((1, tn), lambda j: (0, j))],
        out_specs=pl.BlockSpec((rows, tn), lambda j: (0, j)),
        out_shape=jax.ShapeDtypeStruct((rows, n), F32),
        compiler_params=pltpu.CompilerParams(dimension_semantics=("arbitrary",)),
    )(cp, w, b.reshape(1, n))
    return out[:bsz]


def _proj_kernel(x_ref, mod_ref, g_ref, wT_ref, wc_ref, qg_ref, bf_ref, mc_ref, cw_ref, cb_ref,
                 lng_ref, lnb_ref, bc_ref,
                 qT_ref, k_ref, vT_ref, un_ref,
                 ubuf, carry, tri, *, tk):
    t = pl.program_id(1)
    tm = x_ref.shape[1]
    a_w = N_HEADS * HEAD_DIM
    c_w = un_ref.shape[2]

    @pl.when(t == 0)
    def _():
        ubuf[0:HALO, :] = jnp.zeros((HALO, c_w), F32)
        carry[...] = jnp.zeros_like(carry)
        r = lax.broadcasted_iota(jnp.int32, (tm, tm), 0)
        c = lax.broadcasted_iota(jnp.int32, (tm, tm), 1)
        tri[...] = jnp.where(r <= c, 1.0, 0.0).astype(BF16)

    x = x_ref[0]
    ms = jnp.mean(x * x, axis=-1, keepdims=True)
    h = x * lax.rsqrt(ms + EPS) * g_ref[...]
    h = h * (1.0 + mod_ref[0, 1:2, :]) + mod_ref[0, 0:1, :]
    hb = h.astype(BF16)

    zT = lax.dot_general(wT_ref[...], hb, (((1,), (1,)), ((), ())), preferred_element_type=F32)

    f = zT[3 * a_w:3 * a_w + N_HEADS] + jnp.tile(bf_ref[...], (1, tm // LANES))
    logf = jnp.minimum(f, 0.0) - jnp.log1p(jnp.exp(-jnp.abs(f)))
    parts = jnp.concatenate(list(_split3(logf)) + [jnp.zeros((SUBLANES, tm), F32)], axis=0).astype(BF16)
    within = jnp.dot(parts, tri[...], preferred_element_type=F32)
    total = jnp.dot(parts, jnp.ones((tm, LANES), BF16), preferred_element_type=F32)
    fold = lambda a: a[0:8] + a[8:16] + a[16:24]
    FT = fold(within) + jnp.tile(carry[...], (1, tm // LANES))
    carry[...] = carry[...] + fold(total)
    Fh, Fm, Fl = _split3(FT * LOG2E)

    ri = lax.broadcasted_iota(jnp.int32, (SUBLANES, tm), 0)
    zpad_qk = jnp.zeros((AUG - HEAD_DIM - SUBLANES, tm), F32)
    xv = jnp.where(ri == 0, 1.0, 0.0)
    zpad_v = jnp.zeros((V_ROWS - HEAD_DIM - SUBLANES, tm), F32)
    qg = jnp.tile(qg_ref[...], (1, tm // LANES))
    mc = jnp.tile(mc_ref[...], (1, tm // LANES))
    for hd in range(N_HEADS):
        lo, hi = hd * HEAD_DIM, (hd + 1) * HEAD_DIM
        fh, fm, fl = Fh[hd:hd + 1], Fm[hd:hd + 1], Fl[hd:hd + 1]
        qh = zT[lo:hi]
        qn = qh * lax.rsqrt(jnp.mean(qh * qh, axis=0, keepdims=True) + EPS) * qg
        bound = jnp.sqrt(jnp.sum(qn * qn, axis=0, keepdims=True)) * mc
        xq = jnp.where(ri == 0, fh, jnp.where(ri == 1, fm, jnp.where(ri == 2, fl,
                       jnp.where(ri < 6, 1.0, jnp.where(ri == 6, -bound, 0.0)))))
        qT_ref[0, hd] = jnp.concatenate([qn, xq, zpad_qk], axis=0).astype(BF16)
        kh = zT[a_w + lo:a_w + hi]
        kn = kh * lax.rsqrt(jnp.mean(kh * kh, axis=0, keepdims=True) + EPS)
        xk = jnp.where(ri < 3, 1.0, jnp.where(ri == 3, -fh, jnp.where(ri == 4, -fm,
                       jnp.where(ri == 5, -fl, jnp.where(ri == 6, 1.0, 0.0)))))
        k_ref[0, hd] = jnp.concatenate([kn, xk, zpad_qk], axis=0).T.astype(BF16)
        vaug = jnp.concatenate([zT[2 * a_w + lo:2 * a_w + hi], xv, zpad_v], axis=0).astype(BF16)
        for cidx in range(tm // tk):
            vT_ref[0, hd, cidx] = vaug[:, cidx * tk:(cidx + 1) * tk]

    zc = jnp.dot(hb, wc_ref[...], preferred_element_type=F32)
    ubuf[HALO:HALO + tm, :] = zc[:, :c_w] * _sigmoid(zc[:, c_w:])
    y = jnp.broadcast_to(cb_ref[...], (tm, c_w))
    for j in range(CONV_KERNEL):
        y = y + ubuf[pl.ds(j + HALO - (CONV_KERNEL - 1), tm), :] * cw_ref[j:j + 1, :]
    ubuf[0:HALO, :] = ubuf[tm:tm + HALO, :]
    mu = jnp.mean(y, axis=-1, keepdims=True)
    yc = y - mu
    var = jnp.mean(yc * yc, axis=-1, keepdims=True)
    y = yc * lax.rsqrt(var + EPS) * lng_ref[...] + lnb_ref[...]
    y = y * _sigmoid(y)
    y = y * lax.rsqrt(jnp.mean(y * y, axis=-1, keepdims=True) + EPS) * bc_ref[...]
    un_ref[0] = y.astype(BF16)


def _proj(x, mod3, norm_g, wT, wc, qg, bfb, mcoef, cw, cb, lng, lnb, bc, *, tm, tk):
    bsz, s, d = x.shape
    c_w = cw.shape[1]
    nt = s // tm
    const = lambda shape: pl.BlockSpec(shape, lambda b, t: (0,) * len(shape))
    return pl.pallas_call(
        functools.partial(_proj_kernel, tk=tk),
        grid=(bsz, nt),
        in_specs=[
            pl.BlockSpec((1, tm, d), lambda b, t: (b, t, 0)),
            pl.BlockSpec((1, 6, d), lambda b, t: (b, 0, 0)),
            const((1, d)),
            const(wT.shape),
            const(wc.shape),
            const(qg.shape),
            const(bfb.shape),
            const(mcoef.shape),
            const(cw.shape),
            const((1, c_w)),
            const((1, c_w)),
            const((1, c_w)),
            const((1, c_w)),
        ],
        out_specs=[
            pl.BlockSpec((1, N_HEADS, AUG, tm), lambda b, t: (b, 0, 0, t)),
            pl.BlockSpec((1, N_HEADS, tm, AUG), lambda b, t: (b, 0, t, 0)),
            pl.BlockSpec((1, N_HEADS, tm // tk, V_ROWS, tk), lambda b, t: (b, 0, t, 0, 0)),
            pl.BlockSpec((1, tm, c_w), lambda b, t: (b, t, 0)),
        ],
        out_shape=[
            jax.ShapeDtypeStruct((bsz, N_HEADS, AUG, s), BF16),
            jax.ShapeDtypeStruct((bsz, N_HEADS, s, AUG), BF16),
            jax.ShapeDtypeStruct((bsz, N_HEADS, s // tk, V_ROWS, tk), BF16),
            jax.ShapeDtypeStruct((bsz, s, c_w), BF16),
        ],
        scratch_shapes=[
            pltpu.VMEM((HALO + tm, c_w), F32),
            pltpu.VMEM((N_HEADS, LANES), F32),
            pltpu.VMEM((tm, tm), BF16),
        ],
        compiler_params=pltpu.CompilerParams(
            dimension_semantics=("arbitrary", "arbitrary"), vmem_limit_bytes=VMEM_LIMIT),
    )(x, mod3, norm_g, wT, wc, qg, bfb, mcoef, cw, cb, lng, lnb, bc)


def _attn_kernel(fast_ref, qT_ref, k_ref, vT_ref, o_ref, acc_ref, *, tq, tk, hg):
    i = pl.program_id(2)
    n_full = (i * tq) // tk

    def logits(g, j, masked):
        kt = k_ref[0, g, pl.ds(pl.multiple_of(j * tk, tk), tk), :]
        s = jnp.dot(kt, qT_ref[0, g], preferred_element_type=F32)
        if masked:
            key = j * tk + lax.broadcasted_iota(jnp.int32, (tk, tq), 0)
            qry = i * tq + lax.broadcasted_iota(jnp.int32, (tk, tq), 1)
            s = jnp.where(key <= qry, s, NEG)
        return s

    def finish(g, acc):
        o_ref[0, g * HEAD_DIM:(g + 1) * HEAD_DIM, :] = (
            acc[0:HEAD_DIM] * (1.0 / acc[HEAD_DIM:HEAD_DIM + 1]))

    @pl.when(fast_ref[0] == 1)
    def _():
        def tile(j, masked):
            for g in range(hg):
                p = jnp.exp2(logits(g, j, masked)).astype(BF16)
                acc_ref[g] += jnp.dot(vT_ref[0, g, j], p, preferred_element_type=F32)

        acc_ref[...] = jnp.zeros_like(acc_ref)

        @pl.loop(0, n_full)
        def _(j):
            tile(j, False)

        tile(n_full, True)
        for g in range(hg):
            finish(g, acc_ref[g])

    @pl.when(fast_ref[0] != 1)
    def _():
        for g in range(hg):
            def tile(j, carry, masked):
                m, acc = carry
                s = logits(g, j, masked)
                m_new = jnp.maximum(m, jnp.max(s, axis=0, keepdims=True))
                p = jnp.exp2(s - m_new).astype(BF16)
                acc = acc * jnp.exp2(m - m_new) + jnp.dot(vT_ref[0, g, j], p,
                                                          preferred_element_type=F32)
                return m_new, acc

            carry = (jnp.full((1, tq), NEG, F32), jnp.zeros((V_ROWS, tq), F32))
            carry = lax.fori_loop(0, n_full, lambda j, c: tile(j, c, False), carry)
            finish(g, tile(n_full, carry, True)[1])


def _attention(fast, qT, k, vT, *, tq, tk, hg):
    bsz, nh, _, s = qT.shape
    return pl.pallas_call(
        functools.partial(_attn_kernel, tq=tq, tk=tk, hg=hg),
        grid_spec=pltpu.PrefetchScalarGridSpec(
            num_scalar_prefetch=1,
            grid=(bsz, nh // hg, s // tq),
            in_specs=[
                pl.BlockSpec((1, hg, AUG, tq), lambda b, h, i, f: (b, h, 0, i)),
                pl.BlockSpec((1, hg, s, AUG), lambda b, h, i, f: (b, h, 0, 0)),
                pl.BlockSpec((1, hg, s // tk, V_ROWS, tk), lambda b, h, i, f: (b, h, 0, 0, 0)),
            ],
            out_specs=pl.BlockSpec((1, hg * HEAD_DIM, tq), lambda b, h, i, f: (b, h, i)),
            scratch_shapes=[pltpu.VMEM((hg, V_ROWS, tq), F32)],
        ),
        out_shape=jax.ShapeDtypeStruct((bsz, nh * HEAD_DIM, s), F32),
        compiler_params=pltpu.CompilerParams(
            dimension_semantics=("arbitrary", "arbitrary", "arbitrary"),
            vmem_limit_bytes=VMEM_LIMIT),
    )(fast, qT, k, vT)


def _out_kernel(x_ref, aT_ref, un_ref, mod_ref, ba_ref, g2_ref, wo_ref, w1_ref, w2_ref, o_ref):
    a_w = aT_ref.shape[1]
    d_ff = w1_ref.shape[1]
    aT = aT_ref[0]
    inv = lax.rsqrt(jnp.mean(aT * aT, axis=0, keepdims=True) + EPS)
    an = ((aT * inv).T * ba_ref[...]).astype(BF16)
    o = jnp.dot(an, wo_ref[0:a_w, :], preferred_element_type=F32)
    o = o + jnp.dot(un_ref[0], wo_ref[a_w:, :], preferred_element_type=F32)
    x1 = x_ref[0] + mod_ref[0, 2:3, :] * o
    h = x1 * lax.rsqrt(jnp.mean(x1 * x1, axis=-1, keepdims=True) + EPS) * g2_ref[...]
    hb = (h * (1.0 + mod_ref[0, 4:5, :]) + mod_ref[0, 3:4, :]).astype(BF16)
    y = jnp.zeros_like(x1)
    for c in range(d_ff // FF_CHUNK):
        a = jnp.dot(hb, w1_ref[:, c * FF_CHUNK:(c + 1) * FF_CHUNK], preferred_element_type=F32)
        a = jnp.maximum(a, 0.0)
        y = y + jnp.dot((a * a).astype(BF16), w2_ref[c * FF_CHUNK:(c + 1) * FF_CHUNK, :],
                        preferred_element_type=F32)
    o_ref[0] = x1 + mod_ref[0, 5:6, :] * y


def _out(x, attnT, un, mod3, beta_a, norm_g, wo, w1, w2, *, tm):
    bsz, s, d = x.shape
    a_w = attnT.shape[1]
    c_w = un.shape[2]
    resident = lambda shape: pl.BlockSpec(shape, lambda b, t: (0,) * len(shape),
                                          pipeline_mode=pl.Buffered(1))
    return pl.pallas_call(
        _out_kernel,
        grid=(bsz, s // tm),
        in_specs=[
            pl.BlockSpec((1, tm, d), lambda b, t: (b, t, 0)),
            pl.BlockSpec((1, a_w, tm), lambda b, t: (b, 0, t)),
            pl.BlockSpec((1, tm, c_w), lambda b, t: (b, t, 0)),
            pl.BlockSpec((1, 6, d), lambda b, t: (b, 0, 0)),
            resident((1, a_w)),
            resident((1, d)),
            resident(wo.shape),
            resident(w1.shape),
            resident(w2.shape),
        ],
        out_specs=pl.BlockSpec((1, tm, d), lambda b, t: (b, t, 0)),
        out_shape=jax.ShapeDtypeStruct((bsz, s, d), F32),
        compiler_params=pltpu.CompilerParams(
            dimension_semantics=("arbitrary", "arbitrary"), vmem_limit_bytes=VMEM_LIMIT),
    )(x, attnT, un, mod3, beta_a, norm_g, wo, w1, w2)


def kernel(x, c, w_ada, b_ada, norm1_g, w_in, q_norm_g, k_norm_g, b_f, conv_w, conv_b,
           conv_ln_g, conv_ln_b, beta_attn, beta_conv, w_out, norm2_g, w_ff1, w_ff2):
    bsz, s, d = x.shape
    a_w = N_HEADS * HEAD_DIM
    c_w = conv_w.shape[2]
    assert s % TM_PROJ == 0 and s % TM_OUT == 0 and s % TK == 0 and TK % TQ == 0 and TM_PROJ % TK == 0
    assert w_in.shape[2] == 3 * a_w + N_HEADS + 2 * c_w
    for l in range(w_ada.shape[0]):
        mod3 = _ada(c, w_ada[l], b_ada[l]).reshape(bsz, 6, d)

        w = w_in[l]
        wf = jnp.zeros((d, BF16_ROWS), F32).at[:, :N_HEADS].set(w[:, 3 * a_w:3 * a_w + N_HEADS])
        wT = jnp.concatenate([w[:, :3 * a_w], wf], axis=1).T.astype(BF16)
        wc = w[:, 3 * a_w + N_HEADS:].astype(BF16)
        gain = q_norm_g[l] * k_norm_g[l] * (HEAD_DIM ** -0.5 * LOG2E)
        qg = jnp.broadcast_to(gain[:, None], (HEAD_DIM, LANES))
        bfb = jnp.broadcast_to(b_f[l][:, None], (N_HEADS, LANES))
        row = lambda v: v.reshape(1, -1)

        bound_coef = HEAD_DIM ** 0.5 * BOUND_SLACK
        worst_gap = 2.0 * bound_coef * HEAD_DIM ** 0.5 * jnp.max(jnp.abs(gain))
        fast = worst_gap <= -F32_MIN_EXP2 - 4.0
        mcoef = jnp.broadcast_to(jnp.where(fast, bound_coef, 0.0).astype(F32), (1, LANES))

        qT, k, vT, un = _proj(x, mod3, row(norm1_g[l]), wT, wc, qg, bfb, mcoef, conv_w[l],
                              row(conv_b[l]), row(conv_ln_g[l]), row(conv_ln_b[l]),
                              row(beta_conv[l]), tm=TM_PROJ, tk=TK)
        attnT = _attention(fast.astype(jnp.int32).reshape(1), qT, k, vT, tq=TQ, tk=TK, hg=HEAD_GROUP)
        x = _out(x, attnT, un, mod3, row(beta_attn[l]), row(norm2_g[l]),
                 w_out[l].astype(BF16), w_ff1[l].astype(BF16), w_ff2[l].astype(BF16), tm=TM_OUT)
    return x
```

```python
import functools

import jax
import jax.numpy as jnp
from jax import lax
from jax.experimental import pallas as pl
from jax.experimental.pallas import tpu as pltpu

F32 = jnp.float32
BF16 = jnp.bfloat16

HEAD_DIM = 64
N_HEADS = 8
CONV_KERNEL = 31
EPS = 1e-6

LANES = 128
SUBLANES = 8
BF16_ROWS = 16
AUG = 128
V_ROWS = HEAD_DIM + BF16_ROWS
HALO = 32
NEG = -1e30
LOG2E = 1.4426950408889634
BOUND_SLACK = 1.0 + 2.0 ** -6
F32_MIN_EXP2 = -126.0

TM_PROJ = 512
TQ = 512
TK = 512
HEAD_GROUP = 4
TM_OUT = 512
FF_CHUNK = 1024
VMEM_LIMIT = 56 * 1024 * 1024


def _sigmoid(x):
    return 1.0 / (1.0 + jnp.exp(-x))


def _split3(x):
    hi = x.astype(BF16).astype(F32)
    r = x - hi
    mid = r.astype(BF16).astype(F32)
    lo = (r - mid).astype(BF16).astype(F32)
    return hi, mid, lo


def _ada_kernel(c_ref, w_ref, b_ref, o_ref):
    c = c_ref[...]
    a = c * _sigmoid(c)
    o_ref[...] = jnp.dot(a, w_ref[...], preferred_element_type=F32,
                         precision=lax.Precision.HIGHEST) + b_ref[...]


def _ada(c, w, b):
    bsz, d = c.shape
    n = w.shape[1]
    rows = -(-bsz // SUBLANES) * SUBLANES
    cp = jnp.zeros((rows, d), F32).at[:bsz].set(c)
    tn = 1024
    out = pl.pallas_call(
        _ada_kernel,
        grid=(n // tn,),
        in_specs=[pl.BlockSpec((rows, d), lambda j: (0, 0)),
                  pl.BlockSpec((d, tn), lambda j: (0, j)),
                  pl.BlockSpec((1, tn), lambda j: (0, j))],
        out_specs=pl.BlockSpec((rows, tn), lambda j: (0, j)),
        out_shape=jax.ShapeDtypeStruct((rows, n), F32),
        compiler_params=pltpu.CompilerParams(dimension_semantics=("arbitrary",)),
    )(cp, w, b.reshape(1, n))
    return out[:bsz]


def _proj_kernel(x_ref, mod_ref, g_ref, wT_ref, wc_ref, qg_ref, bf_ref, mc_ref, cw_ref, cb_ref,
                 lng_ref, lnb_ref, bc_ref,
                 qT_ref, k_ref, vT_ref, un_ref,
                 ubuf, shf, carry, tri, *, tk):
    t = pl.program_id(1)
    tm = x_ref.shape[1]
    a_w = N_HEADS * HEAD_DIM
    c_w = un_ref.shape[2]

    @pl.when(t == 0)
    def _():
        ubuf[0:HALO, :] = jnp.zeros((HALO, c_w), F32)
        carry[...] = jnp.zeros_like(carry)
        r = lax.broadcasted_iota(jnp.int32, (tm, tm), 0)
        c = lax.broadcasted_iota(jnp.int32, (tm, tm), 1)
        tri[...] = jnp.where(r <= c, 1.0, 0.0).astype(BF16)

    x = x_ref[0]
    ms = jnp.mean(x * x, axis=-1, keepdims=True)
    h = x * lax.rsqrt(ms + EPS) * g_ref[...]
    h = h * (1.0 + mod_ref[0, 1:2, :]) + mod_ref[0, 0:1, :]
    hb = h.astype(BF16)

    zT = lax.dot_general(wT_ref[...], hb, (((1,), (1,)), ((), ())), preferred_element_type=F32)

    f = zT[3 * a_w:3 * a_w + N_HEADS] + jnp.tile(bf_ref[...], (1, tm // LANES))
    logf = jnp.minimum(f, 0.0) - jnp.log1p(jnp.exp(-jnp.abs(f)))
    parts = jnp.concatenate(list(_split3(logf)) + [jnp.zeros((SUBLANES, tm), F32)], axis=0).astype(BF16)
    within = jnp.dot(parts, tri[...], preferred_element_type=F32)
    total = jnp.dot(parts, jnp.ones((tm, LANES), BF16), preferred_element_type=F32)
    fold = lambda a: a[0:8] + a[8:16] + a[16:24]
    FT = fold(within) + jnp.tile(carry[...], (1, tm // LANES))
    carry[...] = carry[...] + fold(total)
    Fh, Fm, Fl = _split3(FT * LOG2E)

    ri = lax.broadcasted_iota(jnp.int32, (SUBLANES, tm), 0)
    zpad_qk = jnp.zeros((AUG - HEAD_DIM - SUBLANES, tm), F32)
    xv = jnp.where(ri == 0, 1.0, 0.0)
    zpad_v = jnp.zeros((V_ROWS - HEAD_DIM - SUBLANES, tm), F32)
    qg = jnp.tile(qg_ref[...], (1, tm // LANES))
    mc = jnp.tile(mc_ref[...], (1, tm // LANES))
    for hd in range(N_HEADS):
        lo, hi = hd * HEAD_DIM, (hd + 1) * HEAD_DIM
        fh, fm, fl = Fh[hd:hd + 1], Fm[hd:hd + 1], Fl[hd:hd + 1]
        qh = zT[lo:hi]
        qn = qh * lax.rsqrt(jnp.mean(qh * qh, axis=0, keepdims=True) + EPS) * qg
        bound = jnp.sqrt(jnp.sum(qn * qn, axis=0, keepdims=True)) * mc
        xq = jnp.where(ri == 0, fh, jnp.where(ri == 1, fm, jnp.where(ri == 2, fl,
                       jnp.where(ri < 6, 1.0, jnp.where(ri == 6, -bound, 0.0)))))
        qT_ref[0, hd] = jnp.concatenate([qn, xq, zpad_qk], axis=0).astype(BF16)
        kh = zT[a_w + lo:a_w + hi]
        kn = kh * lax.rsqrt(jnp.mean(kh * kh, axis=0, keepdims=True) + EPS)
        xk = jnp.where(ri < 3, 1.0, jnp.where(ri == 3, -fh, jnp.where(ri == 4, -fm,
                       jnp.where(ri == 5, -fl, jnp.where(ri == 6, 1.0, 0.0)))))
        k_ref[0, hd] = jnp.concatenate([kn, xk, zpad_qk], axis=0).T.astype(BF16)
        vaug = jnp.concatenate([zT[2 * a_w + lo:2 * a_w + hi], xv, zpad_v], axis=0).astype(BF16)
        for cidx in range(tm // tk):
            vT_ref[0, hd, cidx] = vaug[:, cidx * tk:(cidx + 1) * tk]

    zc = jnp.dot(hb, wc_ref[...], preferred_element_type=F32)
    ubuf[HALO:HALO + tm, :] = zc[:, :c_w] * _sigmoid(zc[:, c_w:])
    y = jnp.broadcast_to(cb_ref[...], (tm, c_w))
    first = HALO - (CONV_KERNEL - 1)
    for r in range(SUBLANES):
        offs = [o for o in range(first, first + CONV_KERNEL) if o % SUBLANES == r]
        span = max(offs) - r
        src = ubuf
        if r:
            shf[0:tm + span, :] = ubuf[pl.ds(r, tm + span), :]
            src = shf
        for o in offs:
            base = o if src is ubuf else o - r
            y = y + src[pl.ds(base, tm), :] * cw_ref[o - first:o - first + 1, :]
    ubuf[0:HALO, :] = ubuf[tm:tm + HALO, :]
    mu = jnp.mean(y, axis=-1, keepdims=True)
    yc = y - mu
    var = jnp.mean(yc * yc, axis=-1, keepdims=True)
    y = yc * lax.rsqrt(var + EPS) * lng_ref[...] + lnb_ref[...]
    y = y * _sigmoid(y)
    y = y * lax.rsqrt(jnp.mean(y * y, axis=-1, keepdims=True) + EPS) * bc_ref[...]
    un_ref[0] = y.astype(BF16)


def _proj(x, mod3, norm_g, wT, wc, qg, bfb, mcoef, cw, cb, lng, lnb, bc, *, tm, tk):
    bsz, s, d = x.shape
    c_w = cw.shape[1]
    nt = s // tm
    const = lambda shape: pl.BlockSpec(shape, lambda b, t: (0,) * len(shape))
    return pl.pallas_call(
        functools.partial(_proj_kernel, tk=tk),
        grid=(bsz, nt),
        in_specs=[
            pl.BlockSpec((1, tm, d), lambda b, t: (b, t, 0)),
            pl.BlockSpec((1, 6, d), lambda b, t: (b, 0, 0)),
            const((1, d)),
            const(wT.shape),
            const(wc.shape),
            const(qg.shape),
            const(bfb.shape),
            const(mcoef.shape),
            const(cw.shape),
            const((1, c_w)),
            const((1, c_w)),
            const((1, c_w)),
            const((1, c_w)),
        ],
        out_specs=[
            pl.BlockSpec((1, N_HEADS, AUG, tm), lambda b, t: (b, 0, 0, t)),
            pl.BlockSpec((1, N_HEADS, tm, AUG), lambda b, t: (b, 0, t, 0)),
            pl.BlockSpec((1, N_HEADS, tm // tk, V_ROWS, tk), lambda b, t: (b, 0, t, 0, 0)),
            pl.BlockSpec((1, tm, c_w), lambda b, t: (b, t, 0)),
        ],
        out_shape=[
            jax.ShapeDtypeStruct((bsz, N_HEADS, AUG, s), BF16),
            jax.ShapeDtypeStruct((bsz, N_HEADS, s, AUG), BF16),
            jax.ShapeDtypeStruct((bsz, N_HEADS, s // tk, V_ROWS, tk), BF16),
            jax.ShapeDtypeStruct((bsz, s, c_w), BF16),
        ],
        scratch_shapes=[
            pltpu.VMEM((HALO + tm, c_w), F32),
            pltpu.VMEM((HALO + tm, c_w), F32),
            pltpu.VMEM((N_HEADS, LANES), F32),
            pltpu.VMEM((tm, tm), BF16),
        ],
        compiler_params=pltpu.CompilerParams(
            dimension_semantics=("arbitrary", "arbitrary"), vmem_limit_bytes=VMEM_LIMIT),
    )(x, mod3, norm_g, wT, wc, qg, bfb, mcoef, cw, cb, lng, lnb, bc)


def _attn_kernel(fast_ref, qT_ref, k_ref, vT_ref, o_ref, acc_ref, p_ref, *, tq, tk, hg):
    i = pl.program_id(2)
    n_full = (i * tq) // tk

    def logits(g, j, masked):
        kt = k_ref[0, g, pl.ds(pl.multiple_of(j * tk, tk), tk), :]
        s = jnp.dot(kt, qT_ref[0, g], preferred_element_type=F32)
        if masked:
            key = j * tk + lax.broadcasted_iota(jnp.int32, (tk, tq), 0)
            qry = i * tq + lax.broadcasted_iota(jnp.int32, (tk, tq), 1)
            s = jnp.where(key <= qry, s, NEG)
        return s

    def finish(g, acc):
        o_ref[0, g * HEAD_DIM:(g + 1) * HEAD_DIM, :] = (
            acc[0:HEAD_DIM] * (1.0 / acc[HEAD_DIM:HEAD_DIM + 1]))

    @pl.when(fast_ref[0] == 1)
    def _():
        def produce(j, slot):
            for g in range(hg):
                p_ref[slot, g] = jnp.exp2(logits(g, j, True)).astype(BF16)

        def consume(j, slot):
            for g in range(hg):
                acc_ref[g] += jnp.dot(vT_ref[0, g, j], p_ref[slot, g],
                                      preferred_element_type=F32)

        acc_ref[...] = jnp.zeros_like(acc_ref)
        produce(0, 0)

        @pl.loop(0, n_full)
        def _(j):
            slot = j % 2
            consume(j, slot)
            produce(j + 1, 1 - slot)

        consume(n_full, n_full % 2)
        for g in range(hg):
            finish(g, acc_ref[g])

    @pl.when(fast_ref[0] != 1)
    def _():
        for g in range(hg):
            def tile(j, carry, masked):
                m, acc = carry
                s = logits(g, j, masked)
                m_new = jnp.maximum(m, jnp.max(s, axis=0, keepdims=True))
                p = jnp.exp2(s - m_new).astype(BF16)
                acc = acc * jnp.exp2(m - m_new) + jnp.dot(vT_ref[0, g, j], p,
                                                          preferred_element_type=F32)
                return m_new, acc

            carry = (jnp.full((1, tq), NEG, F32), jnp.zeros((V_ROWS, tq), F32))
            carry = lax.fori_loop(0, n_full, lambda j, c: tile(j, c, False), carry)
            finish(g, tile(n_full, carry, True)[1])


def _attention(fast, qT, k, vT, *, tq, tk, hg):
    bsz, nh, _, s = qT.shape
    return pl.pallas_call(
        functools.partial(_attn_kernel, tq=tq, tk=tk, hg=hg),
        grid_spec=pltpu.PrefetchScalarGridSpec(
            num_scalar_prefetch=1,
            grid=(bsz, nh // hg, s // tq),
            in_specs=[
                pl.BlockSpec((1, hg, AUG, tq), lambda b, h, i, f: (b, h, 0, i)),
                pl.BlockSpec((1, hg, s, AUG), lambda b, h, i, f: (b, h, 0, 0),
                             pipeline_mode=pl.Buffered(1)),
                pl.BlockSpec((1, hg, s // tk, V_ROWS, tk), lambda b, h, i, f: (b, h, 0, 0, 0),
                             pipeline_mode=pl.Buffered(1)),
            ],
            out_specs=pl.BlockSpec((1, hg * HEAD_DIM, tq), lambda b, h, i, f: (b, h, i)),
            scratch_shapes=[pltpu.VMEM((hg, V_ROWS, tq), F32),
                            pltpu.VMEM((2, hg, tk, tq), BF16)],
        ),
        out_shape=jax.ShapeDtypeStruct((bsz, nh * HEAD_DIM, s), F32),
        compiler_params=pltpu.CompilerParams(
            dimension_semantics=("arbitrary", "arbitrary", "arbitrary"),
            vmem_limit_bytes=VMEM_LIMIT),
    )(fast, qT, k, vT)


def _out_kernel(x_ref, aT_ref, un_ref, mod_ref, ba_ref, g2_ref, wo_ref, w1_ref, w2_ref, o_ref):
    a_w = aT_ref.shape[1]
    d_ff = w1_ref.shape[1]
    aT = aT_ref[0]
    inv = lax.rsqrt(jnp.mean(aT * aT, axis=0, keepdims=True) + EPS)
    an = ((aT * inv).T * ba_ref[...]).astype(BF16)
    o = jnp.dot(an, wo_ref[0:a_w, :], preferred_element_type=F32)
    o = o + jnp.dot(un_ref[0], wo_ref[a_w:, :], preferred_element_type=F32)
    x1 = x_ref[0] + mod_ref[0, 2:3, :] * o
    h = x1 * lax.rsqrt(jnp.mean(x1 * x1, axis=-1, keepdims=True) + EPS) * g2_ref[...]
    hb = (h * (1.0 + mod_ref[0, 4:5, :]) + mod_ref[0, 3:4, :]).astype(BF16)
    y = jnp.zeros_like(x1)
    for c in range(d_ff // FF_CHUNK):
        a = jnp.dot(hb, w1_ref[:, c * FF_CHUNK:(c + 1) * FF_CHUNK], preferred_element_type=F32)
        a = jnp.maximum(a, 0.0)
        y = y + jnp.dot((a * a).astype(BF16), w2_ref[c * FF_CHUNK:(c + 1) * FF_CHUNK, :],
                        preferred_element_type=F32)
    o_ref[0] = x1 + mod_ref[0, 5:6, :] * y


def _out(x, attnT, un, mod3, beta_a, norm_g, wo, w1, w2, *, tm):
    bsz, s, d = x.shape
    a_w = attnT.shape[1]
    c_w = un.shape[2]
    resident = lambda shape: pl.BlockSpec(shape, lambda b, t: (0,) * len(shape),
                                          pipeline_mode=pl.Buffered(1))
    return pl.pallas_call(
        _out_kernel,
        grid=(bsz, s // tm),
        in_specs=[
            pl.BlockSpec((1, tm, d), lambda b, t: (b, t, 0)),
            pl.BlockSpec((1, a_w, tm), lambda b, t: (b, 0, t)),
            pl.BlockSpec((1, tm, c_w), lambda b, t: (b, t, 0)),
            pl.BlockSpec((1, 6, d), lambda b, t: (b, 0, 0)),
            resident((1, a_w)),
            resident((1, d)),
            resident(wo.shape),
            resident(w1.shape),
            resident(w2.shape),
        ],
        out_specs=pl.BlockSpec((1, tm, d), lambda b, t: (b, t, 0)),
        out_shape=jax.ShapeDtypeStruct((bsz, s, d), F32),
        compiler_params=pltpu.CompilerParams(
            dimension_semantics=("arbitrary", "arbitrary"), vmem_limit_bytes=VMEM_LIMIT),
    )(x, attnT, un, mod3, beta_a, norm_g, wo, w1, w2)


def kernel(x, c, w_ada, b_ada, norm1_g, w_in, q_norm_g, k_norm_g, b_f, conv_w, conv_b,
           conv_ln_g, conv_ln_b, beta_attn, beta_conv, w_out, norm2_g, w_ff1, w_ff2):
    bsz, s, d = x.shape
    a_w = N_HEADS * HEAD_DIM
    c_w = conv_w.shape[2]
    assert s % TM_PROJ == 0 and s % TM_OUT == 0 and s % TK == 0 and TK % TQ == 0 and TM_PROJ % TK == 0
    assert w_in.shape[2] == 3 * a_w + N_HEADS + 2 * c_w
    for l in range(w_ada.shape[0]):
        mod3 = _ada(c, w_ada[l], b_ada[l]).reshape(bsz, 6, d)

        w = w_in[l]
        wf = jnp.zeros((d, BF16_ROWS), F32).at[:, :N_HEADS].set(w[:, 3 * a_w:3 * a_w + N_HEADS])
        wT = jnp.concatenate([w[:, :3 * a_w], wf], axis=1).T.astype(BF16)
        wc = w[:, 3 * a_w + N_HEADS:].astype(BF16)
        gain = q_norm_g[l] * k_norm_g[l] * (HEAD_DIM ** -0.5 * LOG2E)
        qg = jnp.broadcast_to(gain[:, None], (HEAD_DIM, LANES))
        bfb = jnp.broadcast_to(b_f[l][:, None], (N_HEADS, LANES))
        row = lambda v: v.reshape(1, -1)

        bound_coef = HEAD_DIM ** 0.5 * BOUND_SLACK
        worst_gap = 2.0 * bound_coef * HEAD_DIM ** 0.5 * jnp.max(jnp.abs(gain))
        fast = worst_gap <= -F32_MIN_EXP2 - 4.0
        mcoef = jnp.broadcast_to(jnp.where(fast, bound_coef, 0.0).astype(F32), (1, LANES))

        qT, k, vT, un = _proj(x, mod3, row(norm1_g[l]), wT, wc, qg, bfb, mcoef, conv_w[l],
                              row(conv_b[l]), row(conv_ln_g[l]), row(conv_ln_b[l]),
                              row(beta_conv[l]), tm=TM_PROJ, tk=TK)
        attnT = _attention(fast.astype(jnp.int32).reshape(1), qT, k, vT, tq=TQ, tk=TK, hg=HEAD_GROUP)
        x = _out(x, attnT, un, mod3, row(beta_attn[l]), row(norm2_g[l]),
                 w_out[l].astype(BF16), w_ff1[l].astype(BF16), w_ff2[l].astype(BF16), tm=TM_OUT)
    return x
```

```python
import functools

import jax
import jax.numpy as jnp
from jax import lax
from jax.experimental import pallas as pl
from jax.experimental.pallas import tpu as pltpu

F32 = jnp.float32
BF16 = jnp.bfloat16

HEAD_DIM = 64
N_HEADS = 8
CONV_KERNEL = 31
EPS = 1e-6

LANES = 128
SUBLANES = 8
BF16_ROWS = 16
AUG = 128
V_ROWS = HEAD_DIM + BF16_ROWS
HALO = 32
NEG = -1e30
LOG2E = 1.4426950408889634
BOUND_SLACK = 1.0 + 2.0 ** -6
F32_MIN_EXP2 = -126.0

TM_PROJ = 512
TQ = 1024
TK = 1024
TV = 512
HEAD_GROUP = 4
TM_OUT = 512
FF_CHUNK = 1024
VMEM_LIMIT = 56 * 1024 * 1024


def _sigmoid(x):
    return 1.0 / (1.0 + jnp.exp(-x))


def _split3(x):
    hi = x.astype(BF16).astype(F32)
    r = x - hi
    mid = r.astype(BF16).astype(F32)
    lo = (r - mid).astype(BF16).astype(F32)
    return hi, mid, lo


def _ada_kernel(c_ref, w_ref, b_ref, o_ref):
    c = c_ref[...]
    a = c * _sigmoid(c)
    o_ref[...] = jnp.dot(a, w_ref[...], preferred_element_type=F32,
                         precision=lax.Precision.HIGHEST) + b_ref[...]


def _ada(c, w, b):
    bsz, d = c.shape
    n = w.shape[1]
    rows = -(-bsz // SUBLANES) * SUBLANES
    cp = jnp.zeros((rows, d), F32).at[:bsz].set(c)
    tn = 1024
    out = pl.pallas_call(
        _ada_kernel,
        grid=(n // tn,),
        in_specs=[pl.BlockSpec((rows, d), lambda j: (0, 0)),
                  pl.BlockSpec((d, tn), lambda j: (0, j)),
                  pl.BlockSpec((1, tn), lambda j: (0, j))],
        out_specs=pl.BlockSpec((rows, tn), lambda j: (0, j)),
        out_shape=jax.ShapeDtypeStruct((rows, n), F32),
        compiler_params=pltpu.CompilerParams(dimension_semantics=("arbitrary",)),
    )(cp, w, b.reshape(1, n))
    return out[:bsz]


def _proj_kernel(x_ref, mod_ref, g_ref, wT_ref, wc_ref, qg_ref, bf_ref, mc_ref, cw_ref, cb_ref,
                 lng_ref, lnb_ref, bc_ref,
                 qT_ref, k_ref, vT_ref, un_ref,
                 ubuf, shf, carry, tri, *, tv):
    t = pl.program_id(1)
    tm = x_ref.shape[1]
    a_w = N_HEADS * HEAD_DIM
    c_w = un_ref.shape[2]

    @pl.when(t == 0)
    def _():
        ubuf[0:HALO, :] = jnp.zeros((HALO, c_w), F32)
        carry[...] = jnp.zeros_like(carry)
        r = lax.broadcasted_iota(jnp.int32, (tm, tm), 0)
        c = lax.broadcasted_iota(jnp.int32, (tm, tm), 1)
        tri[...] = jnp.where(r <= c, 1.0, 0.0).astype(BF16)

    x = x_ref[0]
    ms = jnp.mean(x * x, axis=-1, keepdims=True)
    h = x * lax.rsqrt(ms + EPS) * g_ref[...]
    h = h * (1.0 + mod_ref[0, 1:2, :]) + mod_ref[0, 0:1, :]
    hb = h.astype(BF16)

    zT = lax.dot_general(wT_ref[...], hb, (((1,), (1,)), ((), ())), preferred_element_type=F32)

    f = zT[3 * a_w:3 * a_w + N_HEADS] + jnp.tile(bf_ref[...], (1, tm // LANES))
    logf = jnp.minimum(f, 0.0) - jnp.log1p(jnp.exp(-jnp.abs(f)))
    parts = jnp.concatenate(list(_split3(logf)) + [jnp.zeros((SUBLANES, tm), F32)], axis=0).astype(BF16)
    within = jnp.dot(parts, tri[...], preferred_element_type=F32)
    total = jnp.dot(parts, jnp.ones((tm, LANES), BF16), preferred_element_type=F32)
    fold = lambda a: a[0:8] + a[8:16] + a[16:24]
    FT = fold(within) + jnp.tile(carry[...], (1, tm // LANES))
    carry[...] = carry[...] + fold(total)
    Fh, Fm, Fl = _split3(FT * LOG2E)

    ri = lax.broadcasted_iota(jnp.int32, (SUBLANES, tm), 0)
    zpad_qk = jnp.zeros((AUG - HEAD_DIM - SUBLANES, tm), F32)
    xv = jnp.where(ri == 0, 1.0, 0.0)
    zpad_v = jnp.zeros((V_ROWS - HEAD_DIM - SUBLANES, tm), F32)
    qg = jnp.tile(qg_ref[...], (1, tm // LANES))
    mc = jnp.tile(mc_ref[...], (1, tm // LANES))
    for hd in range(N_HEADS):
        lo, hi = hd * HEAD_DIM, (hd + 1) * HEAD_DIM
        fh, fm, fl = Fh[hd:hd + 1], Fm[hd:hd + 1], Fl[hd:hd + 1]
        qh = zT[lo:hi]
        qn = qh * lax.rsqrt(jnp.mean(qh * qh, axis=0, keepdims=True) + EPS) * qg
        bound = jnp.sqrt(jnp.sum(qn * qn, axis=0, keepdims=True)) * mc
        xq = jnp.where(ri == 0, fh, jnp.where(ri == 1, fm, jnp.where(ri == 2, fl,
                       jnp.where(ri < 6, 1.0, jnp.where(ri == 6, -bound, 0.0)))))
        qT_ref[0, hd] = jnp.concatenate([qn, xq, zpad_qk], axis=0).astype(BF16)
        kh = zT[a_w + lo:a_w + hi]
        kn = kh * lax.rsqrt(jnp.mean(kh * kh, axis=0, keepdims=True) + EPS)
        xk = jnp.where(ri < 3, 1.0, jnp.where(ri == 3, -fh, jnp.where(ri == 4, -fm,
                       jnp.where(ri == 5, -fl, jnp.where(ri == 6, 1.0, 0.0)))))
        k_ref[0, hd] = jnp.concatenate([kn, xk, zpad_qk], axis=0).T.astype(BF16)
        vaug = jnp.concatenate([zT[2 * a_w + lo:2 * a_w + hi], xv, zpad_v], axis=0).astype(BF16)
        for cidx in range(tm // tv):
            vT_ref[0, hd, cidx] = vaug[:, cidx * tv:(cidx + 1) * tv]

    zc = jnp.dot(hb, wc_ref[...], preferred_element_type=F32)
    ubuf[HALO:HALO + tm, :] = zc[:, :c_w] * _sigmoid(zc[:, c_w:])
    y = jnp.broadcast_to(cb_ref[...], (tm, c_w))
    first = HALO - (CONV_KERNEL - 1)
    for r in range(SUBLANES):
        offs = [o for o in range(first, first + CONV_KERNEL) if o % SUBLANES == r]
        span = max(offs) - r
        src = ubuf
        if r:
            shf[0:tm + span, :] = ubuf[pl.ds(r, tm + span), :]
            src = shf
        for o in offs:
            base = o if src is ubuf else o - r
            y = y + src[pl.ds(base, tm), :] * cw_ref[o - first:o - first + 1, :]
    ubuf[0:HALO, :] = ubuf[tm:tm + HALO, :]
    mu = jnp.mean(y, axis=-1, keepdims=True)
    yc = y - mu
    var = jnp.mean(yc * yc, axis=-1, keepdims=True)
    y = yc * lax.rsqrt(var + EPS) * lng_ref[...] + lnb_ref[...]
    y = y * _sigmoid(y)
    y = y * lax.rsqrt(jnp.mean(y * y, axis=-1, keepdims=True) + EPS) * bc_ref[...]
    un_ref[0] = y.astype(BF16)


def _proj(x, mod3, norm_g, wT, wc, qg, bfb, mcoef, cw, cb, lng, lnb, bc, *, tm, tv):
    bsz, s, d = x.shape
    c_w = cw.shape[1]
    nt = s // tm
    const = lambda shape: pl.BlockSpec(shape, lambda b, t: (0,) * len(shape))
    return pl.pallas_call(
        functools.partial(_proj_kernel, tv=tv),
        grid=(bsz, nt),
        in_specs=[
            pl.BlockSpec((1, tm, d), lambda b, t: (b, t, 0)),
            pl.BlockSpec((1, 6, d), lambda b, t: (b, 0, 0)),
            const((1, d)),
            const(wT.shape),
            const(wc.shape),
            const(qg.shape),
            const(bfb.shape),
            const(mcoef.shape),
            const(cw.shape),
            const((1, c_w)),
            const((1, c_w)),
            const((1, c_w)),
            const((1, c_w)),
        ],
        out_specs=[
            pl.BlockSpec((1, N_HEADS, AUG, tm), lambda b, t: (b, 0, 0, t)),
            pl.BlockSpec((1, N_HEADS, tm, AUG), lambda b, t: (b, 0, t, 0)),
            pl.BlockSpec((1, N_HEADS, tm // tv, V_ROWS, tv), lambda b, t: (b, 0, t, 0, 0)),
            pl.BlockSpec((1, tm, c_w), lambda b, t: (b, t, 0)),
        ],
        out_shape=[
            jax.ShapeDtypeStruct((bsz, N_HEADS, AUG, s), BF16),
            jax.ShapeDtypeStruct((bsz, N_HEADS, s, AUG), BF16),
            jax.ShapeDtypeStruct((bsz, N_HEADS, s // tv, V_ROWS, tv), BF16),
            jax.ShapeDtypeStruct((bsz, s, c_w), BF16),
        ],
        scratch_shapes=[
            pltpu.VMEM((HALO + tm, c_w), F32),
            pltpu.VMEM((HALO + tm, c_w), F32),
            pltpu.VMEM((N_HEADS, LANES), F32),
            pltpu.VMEM((tm, tm), BF16),
        ],
        compiler_params=pltpu.CompilerParams(
            dimension_semantics=("arbitrary", "arbitrary"), vmem_limit_bytes=VMEM_LIMIT),
    )(x, mod3, norm_g, wT, wc, qg, bfb, mcoef, cw, cb, lng, lnb, bc)


def _attn_kernel(fast_ref, qT_ref, k_ref, vT_ref, o_ref, acc_ref, p_ref, *, tq, tk, hg):
    i = pl.program_id(2)
    n_full = (i * tq) // tk
    n_diag = max(tq // tk, 1)
    n_last = n_full + n_diag - 1
    tv = vT_ref.shape[4]

    def pv(g, j, p):
        out = None
        for c in range(tk // tv):
            part = jnp.dot(vT_ref[0, g, j * (tk // tv) + c], p[c * tv:(c + 1) * tv],
                           preferred_element_type=F32)
            out = part if out is None else out + part
        return out

    def logits(g, j, masked):
        kt = k_ref[0, g, pl.ds(pl.multiple_of(j * tk, tk), tk), :]
        s = jnp.dot(kt, qT_ref[0, g], preferred_element_type=F32)
        if masked:
            key = j * tk + lax.broadcasted_iota(jnp.int32, (tk, tq), 0)
            qry = i * tq + lax.broadcasted_iota(jnp.int32, (tk, tq), 1)
            s = jnp.where(key <= qry, s, NEG)
        return s

    def finish(g, acc):
        o_ref[0, g * HEAD_DIM:(g + 1) * HEAD_DIM, :] = (
            acc[0:HEAD_DIM] * (1.0 / acc[HEAD_DIM:HEAD_DIM + 1]))

    @pl.when(fast_ref[0] == 1)
    def _():
        def produce(j, slot):
            for g in range(hg):
                p_ref[slot, g] = jnp.exp2(logits(g, j, True)).astype(BF16)

        def consume(j, slot):
            for g in range(hg):
                acc_ref[g] += pv(g, j, p_ref[slot, g])

        acc_ref[...] = jnp.zeros_like(acc_ref)
        produce(0, 0)

        @pl.loop(0, n_last)
        def _(j):
            slot = j % 2
            consume(j, slot)
            produce(j + 1, 1 - slot)

        consume(n_last, n_last % 2)
        for g in range(hg):
            finish(g, acc_ref[g])

    @pl.when(fast_ref[0] != 1)
    def _():
        for g in range(hg):
            def tile(j, carry, masked):
                m, acc = carry
                s = logits(g, j, masked)
                m_new = jnp.maximum(m, jnp.max(s, axis=0, keepdims=True))
                p = jnp.exp2(s - m_new).astype(BF16)
                acc = acc * jnp.exp2(m - m_new) + pv(g, j, p)
                return m_new, acc

            carry = (jnp.full((1, tq), NEG, F32), jnp.zeros((V_ROWS, tq), F32))
            carry = lax.fori_loop(0, n_full, lambda j, c: tile(j, c, False), carry)
            for d in range(n_diag):
                carry = tile(n_full + d, carry, True)
            finish(g, carry[1])


def _attention(fast, qT, k, vT, *, tq, tk, hg):
    bsz, nh, _, s = qT.shape
    tv = vT.shape[4]
    return pl.pallas_call(
        functools.partial(_attn_kernel, tq=tq, tk=tk, hg=hg),
        grid_spec=pltpu.PrefetchScalarGridSpec(
            num_scalar_prefetch=1,
            grid=(bsz, nh // hg, s // tq),
            in_specs=[
                pl.BlockSpec((1, hg, AUG, tq), lambda b, h, i, f: (b, h, 0, i)),
                pl.BlockSpec((1, hg, s, AUG), lambda b, h, i, f: (b, h, 0, 0),
                             pipeline_mode=pl.Buffered(1)),
                pl.BlockSpec((1, hg, s // tv, V_ROWS, tv), lambda b, h, i, f: (b, h, 0, 0, 0),
                             pipeline_mode=pl.Buffered(1)),
            ],
            out_specs=pl.BlockSpec((1, hg * HEAD_DIM, tq), lambda b, h, i, f: (b, h, i)),
            scratch_shapes=[pltpu.VMEM((hg, V_ROWS, tq), F32),
                            pltpu.VMEM((2, hg, tk, tq), BF16)],
        ),
        out_shape=jax.ShapeDtypeStruct((bsz, nh * HEAD_DIM, s), F32),
        compiler_params=pltpu.CompilerParams(
            dimension_semantics=("arbitrary", "arbitrary", "arbitrary"),
            vmem_limit_bytes=VMEM_LIMIT),
    )(fast, qT, k, vT)


def _out_kernel(x_ref, aT_ref, un_ref, mod_ref, ba_ref, g2_ref, wo_ref, w1_ref, w2_ref, o_ref):
    a_w = aT_ref.shape[1]
    d_ff = w1_ref.shape[1]
    aT = aT_ref[0]
    inv = lax.rsqrt(jnp.mean(aT * aT, axis=0, keepdims=True) + EPS)
    an = ((aT * inv).T * ba_ref[...]).astype(BF16)
    o = jnp.dot(an, wo_ref[0:a_w, :], preferred_element_type=F32)
    o = o + jnp.dot(un_ref[0], wo_ref[a_w:, :], preferred_element_type=F32)
    x1 = x_ref[0] + mod_ref[0, 2:3, :] * o
    h = x1 * lax.rsqrt(jnp.mean(x1 * x1, axis=-1, keepdims=True) + EPS) * g2_ref[...]
    hb = (h * (1.0 + mod_ref[0, 4:5, :]) + mod_ref[0, 3:4, :]).astype(BF16)
    y = jnp.zeros_like(x1)
    for c in range(d_ff // FF_CHUNK):
        a = jnp.dot(hb, w1_ref[:, c * FF_CHUNK:(c + 1) * FF_CHUNK], preferred_element_type=F32)
        a = jnp.maximum(a, 0.0)
        y = y + jnp.dot((a * a).astype(BF16), w2_ref[c * FF_CHUNK:(c + 1) * FF_CHUNK, :],
                        preferred_element_type=F32)
    o_ref[0] = x1 + mod_ref[0, 5:6, :] * y


def _out(x, attnT, un, mod3, beta_a, norm_g, wo, w1, w2, *, tm):
    bsz, s, d = x.shape
    a_w = attnT.shape[1]
    c_w = un.shape[2]
    resident = lambda shape: pl.BlockSpec(shape, lambda b, t: (0,) * len(shape),
                                          pipeline_mode=pl.Buffered(1))
    return pl.pallas_call(
        _out_kernel,
        grid=(bsz, s // tm),
        in_specs=[
            pl.BlockSpec((1, tm, d), lambda b, t: (b, t, 0)),
            pl.BlockSpec((1, a_w, tm), lambda b, t: (b, 0, t)),
            pl.BlockSpec((1, tm, c_w), lambda b, t: (b, t, 0)),
            pl.BlockSpec((1, 6, d), lambda b, t: (b, 0, 0)),
            resident((1, a_w)),
            resident((1, d)),
            resident(wo.shape),
            resident(w1.shape),
            resident(w2.shape),
        ],
        out_specs=pl.BlockSpec((1, tm, d), lambda b, t: (b, t, 0)),
        out_shape=jax.ShapeDtypeStruct((bsz, s, d), F32),
        compiler_params=pltpu.CompilerParams(
            dimension_semantics=("arbitrary", "arbitrary"), vmem_limit_bytes=VMEM_LIMIT),
    )(x, attnT, un, mod3, beta_a, norm_g, wo, w1, w2)


def kernel(x, c, w_ada, b_ada, norm1_g, w_in, q_norm_g, k_norm_g, b_f, conv_w, conv_b,
           conv_ln_g, conv_ln_b, beta_attn, beta_conv, w_out, norm2_g, w_ff1, w_ff2):
    bsz, s, d = x.shape
    a_w = N_HEADS * HEAD_DIM
    c_w = conv_w.shape[2]
    assert s % TM_PROJ == 0 and s % TM_OUT == 0 and s % TK == 0 and s % TQ == 0
    assert (TK % TQ == 0 or TQ % TK == 0) and TM_PROJ % TV == 0 and TK % TV == 0
    assert w_in.shape[2] == 3 * a_w + N_HEADS + 2 * c_w
    for l in range(w_ada.shape[0]):
        mod3 = _ada(c, w_ada[l], b_ada[l]).reshape(bsz, 6, d)

        w = w_in[l]
        wf = jnp.zeros((d, BF16_ROWS), F32).at[:, :N_HEADS].set(w[:, 3 * a_w:3 * a_w + N_HEADS])
        wT = jnp.concatenate([w[:, :3 * a_w], wf], axis=1).T.astype(BF16)
        wc = w[:, 3 * a_w + N_HEADS:].astype(BF16)
        gain = q_norm_g[l] * k_norm_g[l] * (HEAD_DIM ** -0.5 * LOG2E)
        qg = jnp.broadcast_to(gain[:, None], (HEAD_DIM, LANES))
        bfb = jnp.broadcast_to(b_f[l][:, None], (N_HEADS, LANES))
        row = lambda v: v.reshape(1, -1)

        bound_coef = HEAD_DIM ** 0.5 * BOUND_SLACK
        worst_gap = 2.0 * bound_coef * HEAD_DIM ** 0.5 * jnp.max(jnp.abs(gain))
        fast = worst_gap <= -F32_MIN_EXP2 - 4.0
        mcoef = jnp.broadcast_to(jnp.where(fast, bound_coef, 0.0).astype(F32), (1, LANES))

        qT, k, vT, un = _proj(x, mod3, row(norm1_g[l]), wT, wc, qg, bfb, mcoef, conv_w[l],
                              row(conv_b[l]), row(conv_ln_g[l]), row(conv_ln_b[l]),
                              row(beta_conv[l]), tm=TM_PROJ, tv=TV)
        attnT = _attention(fast.astype(jnp.int32).reshape(1), qT, k, vT, tq=TQ, tk=TK, hg=HEAD_GROUP)
        x = _out(x, attnT, un, mod3, row(beta_attn[l]), row(norm2_g[l]),
                 w_out[l].astype(BF16), w_ff1[l].astype(BF16), w_ff2[l].astype(BF16), tm=TM_OUT)
    return x
```

```python
import functools

import jax
import jax.numpy as jnp
from jax import lax
from jax.experimental import pallas as pl
from jax.experimental.pallas import tpu as pltpu

F32 = jnp.float32
BF16 = jnp.bfloat16

HEAD_DIM = 64
N_HEADS = 8
CONV_KERNEL = 31
EPS = 1e-6

LANES = 128
SUBLANES = 8
BF16_ROWS = 16
AUG = 128
V_ROWS = HEAD_DIM + BF16_ROWS
HALO = 32
NEG = -1e30
LOG2E = 1.4426950408889634
BOUND_SLACK = 1.0 + 2.0 ** -6
F32_MIN_EXP2 = -126.0
SKIP_GAP = 160.0

TM_PROJ = 512
TQ = 1024
TK = 1024
TV = 512
HEAD_GROUP = 2
TM_OUT = 512
FF_CHUNK = 1024
VMEM_LIMIT = 56 * 1024 * 1024


def _sigmoid(x):
    return 1.0 / (1.0 + jnp.exp(-x))


def _split3(x):
    hi = x.astype(BF16).astype(F32)
    r = x - hi
    mid = r.astype(BF16).astype(F32)
    lo = (r - mid).astype(BF16).astype(F32)
    return hi, mid, lo


def _ada_kernel(c_ref, w_ref, b_ref, o_ref):
    c = c_ref[...]
    a = c * _sigmoid(c)
    o_ref[...] = jnp.dot(a, w_ref[...], preferred_element_type=F32,
                         precision=lax.Precision.HIGHEST) + b_ref[...]


def _ada(c, w, b):
    bsz, d = c.shape
    n = w.shape[1]
    rows = -(-bsz // SUBLANES) * SUBLANES
    cp = jnp.zeros((rows, d), F32).at[:bsz].set(c)
    tn = 1024
    out = pl.pallas_call(
        _ada_kernel,
        grid=(n // tn,),
        in_specs=[pl.BlockSpec((rows, d), lambda j: (0, 0)),
                  pl.BlockSpec((d, tn), lambda j: (0, j)),
                  pl.BlockSpec((1, tn), lambda j: (0, j))],
        out_specs=pl.BlockSpec((rows, tn), lambda j: (0, j)),
        out_shape=jax.ShapeDtypeStruct((rows, n), F32),
        compiler_params=pltpu.CompilerParams(dimension_semantics=("arbitrary",)),
    )(cp, w, b.reshape(1, n))
    return out[:bsz]


def _proj_kernel(x_ref, mod_ref, g_ref, wT_ref, wc_ref, qg_ref, bf_ref, mc_ref, cw_ref, cb_ref,
                 lng_ref, lnb_ref, bc_ref,
                 qT_ref, k_ref, vT_ref, un_ref, f2_ref,
                 ubuf, shf, carry, tri, *, tv):
    t = pl.program_id(1)
    tm = x_ref.shape[1]
    a_w = N_HEADS * HEAD_DIM
    c_w = un_ref.shape[2]

    @pl.when(t == 0)
    def _():
        ubuf[0:HALO, :] = jnp.zeros((HALO, c_w), F32)
        carry[...] = jnp.zeros_like(carry)
        r = lax.broadcasted_iota(jnp.int32, (tm, tm), 0)
        c = lax.broadcasted_iota(jnp.int32, (tm, tm), 1)
        tri[...] = jnp.where(r <= c, 1.0, 0.0).astype(BF16)

    x = x_ref[0]
    ms = jnp.mean(x * x, axis=-1, keepdims=True)
    h = x * lax.rsqrt(ms + EPS) * g_ref[...]
    h = h * (1.0 + mod_ref[0, 1:2, :]) + mod_ref[0, 0:1, :]
    hb = h.astype(BF16)

    zT = lax.dot_general(wT_ref[...], hb, (((1,), (1,)), ((), ())), preferred_element_type=F32)

    f = zT[3 * a_w:3 * a_w + N_HEADS] + jnp.tile(bf_ref[...], (1, tm // LANES))
    logf = jnp.minimum(f, 0.0) - jnp.log1p(jnp.exp(-jnp.abs(f)))
    parts = jnp.concatenate(list(_split3(logf)) + [jnp.zeros((SUBLANES, tm), F32)], axis=0).astype(BF16)
    within = jnp.dot(parts, tri[...], preferred_element_type=F32)
    total = jnp.dot(parts, jnp.ones((tm, LANES), BF16), preferred_element_type=F32)
    fold = lambda a: a[0:8] + a[8:16] + a[16:24]
    FT = fold(within) + jnp.tile(carry[...], (1, tm // LANES))
    carry[...] = carry[...] + fold(total)
    F2 = FT * LOG2E
    f2_ref[0] = F2
    Fh, Fm, Fl = _split3(F2)

    ri = lax.broadcasted_iota(jnp.int32, (SUBLANES, tm), 0)
    zpad_qk = jnp.zeros((AUG - HEAD_DIM - SUBLANES, tm), F32)
    xv = jnp.where(ri == 0, 1.0, 0.0)
    zpad_v = jnp.zeros((V_ROWS - HEAD_DIM - SUBLANES, tm), F32)
    qg = jnp.tile(qg_ref[...], (1, tm // LANES))
    mc = jnp.tile(mc_ref[...], (1, tm // LANES))
    for hd in range(N_HEADS):
        lo, hi = hd * HEAD_DIM, (hd + 1) * HEAD_DIM
        fh, fm, fl = Fh[hd:hd + 1], Fm[hd:hd + 1], Fl[hd:hd + 1]
        qh = zT[lo:hi]
        qn = qh * lax.rsqrt(jnp.mean(qh * qh, axis=0, keepdims=True) + EPS) * qg
        bound = jnp.sqrt(jnp.sum(qn * qn, axis=0, keepdims=True)) * mc
        xq = jnp.where(ri == 0, fh, jnp.where(ri == 1, fm, jnp.where(ri == 2, fl,
                       jnp.where(ri < 6, 1.0, jnp.where(ri == 6, -bound, 0.0)))))
        qT_ref[0, hd] = jnp.concatenate([qn, xq, zpad_qk], axis=0).astype(BF16)
        kh = zT[a_w + lo:a_w + hi]
        kn = kh * lax.rsqrt(jnp.mean(kh * kh, axis=0, keepdims=True) + EPS)
        xk = jnp.where(ri < 3, 1.0, jnp.where(ri == 3, -fh, jnp.where(ri == 4, -fm,
                       jnp.where(ri == 5, -fl, jnp.where(ri == 6, 1.0, 0.0)))))
        k_ref[0, hd] = jnp.concatenate([kn, xk, zpad_qk], axis=0).T.astype(BF16)
        vaug = jnp.concatenate([zT[2 * a_w + lo:2 * a_w + hi], xv, zpad_v], axis=0).astype(BF16)
        for cidx in range(tm // tv):
            vT_ref[0, hd, cidx] = vaug[:, cidx * tv:(cidx + 1) * tv]

    zc = jnp.dot(hb, wc_ref[...], preferred_element_type=F32)
    ubuf[HALO:HALO + tm, :] = zc[:, :c_w] * _sigmoid(zc[:, c_w:])
    y = jnp.broadcast_to(cb_ref[...], (tm, c_w))
    first = HALO - (CONV_KERNEL - 1)
    for r in range(SUBLANES):
        offs = [o for o in range(first, first + CONV_KERNEL) if o % SUBLANES == r]
        span = max(offs) - r
        src = ubuf
        if r:
            shf[0:tm + span, :] = ubuf[pl.ds(r, tm + span), :]
            src = shf
        for o in offs:
            base = o if src is ubuf else o - r
            y = y + src[pl.ds(base, tm), :] * cw_ref[o - first:o - first + 1, :]
    ubuf[0:HALO, :] = ubuf[tm:tm + HALO, :]
    mu = jnp.mean(y, axis=-1, keepdims=True)
    yc = y - mu
    var = jnp.mean(yc * yc, axis=-1, keepdims=True)
    y = yc * lax.rsqrt(var + EPS) * lng_ref[...] + lnb_ref[...]
    y = y * _sigmoid(y)
    y = y * lax.rsqrt(jnp.mean(y * y, axis=-1, keepdims=True) + EPS) * bc_ref[...]
    un_ref[0] = y.astype(BF16)


def _proj(x, mod3, norm_g, wT, wc, qg, bfb, mcoef, cw, cb, lng, lnb, bc, *, tm, tv):
    bsz, s, d = x.shape
    c_w = cw.shape[1]
    nt = s // tm
    const = lambda shape: pl.BlockSpec(shape, lambda b, t: (0,) * len(shape))
    return pl.pallas_call(
        functools.partial(_proj_kernel, tv=tv),
        grid=(bsz, nt),
        in_specs=[
            pl.BlockSpec((1, tm, d), lambda b, t: (b, t, 0)),
            pl.BlockSpec((1, 6, d), lambda b, t: (b, 0, 0)),
            const((1, d)),
            const(wT.shape),
            const(wc.shape),
            const(qg.shape),
            const(bfb.shape),
            const(mcoef.shape),
            const(cw.shape),
            const((1, c_w)),
            const((1, c_w)),
            const((1, c_w)),
            const((1, c_w)),
        ],
        out_specs=[
            pl.BlockSpec((1, N_HEADS, AUG, tm), lambda b, t: (b, 0, 0, t)),
            pl.BlockSpec((1, N_HEADS, tm, AUG), lambda b, t: (b, 0, t, 0)),
            pl.BlockSpec((1, N_HEADS, tm // tv, V_ROWS, tv), lambda b, t: (b, 0, t, 0, 0)),
            pl.BlockSpec((1, tm, c_w), lambda b, t: (b, t, 0)),
            pl.BlockSpec((1, N_HEADS, tm), lambda b, t: (b, 0, t)),
        ],
        out_shape=[
            jax.ShapeDtypeStruct((bsz, N_HEADS, AUG, s), BF16),
            jax.ShapeDtypeStruct((bsz, N_HEADS, s, AUG), BF16),
            jax.ShapeDtypeStruct((bsz, N_HEADS, s // tv, V_ROWS, tv), BF16),
            jax.ShapeDtypeStruct((bsz, s, c_w), BF16),
            jax.ShapeDtypeStruct((bsz, N_HEADS, s), F32),
        ],
        scratch_shapes=[
            pltpu.VMEM((HALO + tm, c_w), F32),
            pltpu.VMEM((HALO + tm, c_w), F32),
            pltpu.VMEM((N_HEADS, LANES), F32),
            pltpu.VMEM((tm, tm), BF16),
        ],
        compiler_params=pltpu.CompilerParams(
            dimension_semantics=("arbitrary", "arbitrary"), vmem_limit_bytes=VMEM_LIMIT),
    )(x, mod3, norm_g, wT, wc, qg, bfb, mcoef, cw, cb, lng, lnb, bc)


def _attn_kernel(fast_ref, perm_ref, first_ref, *refs, tq, tk, hg):
    del perm_ref
    qT_refs, k_refs, vT_refs = refs[0:hg], refs[hg:2 * hg], refs[2 * hg:3 * hg]
    o_ref, acc_ref, p_ref = refs[3 * hg:]
    i = pl.program_id(2)
    n_full = (i * tq) // tk
    n_diag = max(tq // tk, 1)
    n_last = n_full + n_diag - 1
    tv = vT_refs[0].shape[4]
    j0 = first_ref[(pl.program_id(0) * pl.num_programs(1) + pl.program_id(1)) * pl.num_programs(2) + i]

    def pv(g, j, p):
        out = None
        for c in range(tk // tv):
            part = jnp.dot(vT_refs[g][0, 0, j * (tk // tv) + c], p[c * tv:(c + 1) * tv],
                           preferred_element_type=F32)
            out = part if out is None else out + part
        return out

    def logits(g, j, masked):
        kt = k_refs[g][0, 0, pl.ds(pl.multiple_of(j * tk, tk), tk), :]
        s = jnp.dot(kt, qT_refs[g][0, 0], preferred_element_type=F32)
        if masked:
            key = j * tk + lax.broadcasted_iota(jnp.int32, (tk, tq), 0)
            qry = i * tq + lax.broadcasted_iota(jnp.int32, (tk, tq), 1)
            s = jnp.where(key <= qry, s, NEG)
        return s

    def finish(g, acc):
        o_ref[0, g * HEAD_DIM:(g + 1) * HEAD_DIM, :] = (
            acc[0:HEAD_DIM] * (1.0 / acc[HEAD_DIM:HEAD_DIM + 1]))

    @pl.when(fast_ref[0] == 1)
    def _():
        def produce(j, slot):
            for g in range(hg):
                p_ref[slot, g] = jnp.exp2(logits(g, j, True)).astype(BF16)

        def consume(j, slot):
            for g in range(hg):
                acc_ref[g] += pv(g, j, p_ref[slot, g])

        acc_ref[...] = jnp.zeros_like(acc_ref)
        produce(j0, 0)

        @pl.loop(j0, n_last)
        def _(j):
            slot = (j - j0) % 2
            consume(j, slot)
            produce(j + 1, 1 - slot)

        consume(n_last, (n_last - j0) % 2)
        for g in range(hg):
            finish(g, acc_ref[g])

    @pl.when(fast_ref[0] != 1)
    def _():
        for g in range(hg):
            def tile(j, carry, masked):
                m, acc = carry
                s = logits(g, j, masked)
                m_new = jnp.maximum(m, jnp.max(s, axis=0, keepdims=True))
                p = jnp.exp2(s - m_new).astype(BF16)
                acc = acc * jnp.exp2(m - m_new) + pv(g, j, p)
                return m_new, acc

            carry = (jnp.full((1, tq), NEG, F32), jnp.zeros((V_ROWS, tq), F32))
            carry = lax.fori_loop(0, n_full, lambda j, c: tile(j, c, False), carry)
            for d in range(n_diag):
                carry = tile(n_full + d, carry, True)
            finish(g, carry[1])


def _attention(fast, perm, first, qT, k, vT, *, tq, tk, hg):
    bsz, nh, _, s = qT.shape
    tv = vT.shape[4]
    head = lambda h, g, pe: pe[h * hg + g]
    q_specs = [pl.BlockSpec((1, 1, AUG, tq),
                            lambda b, h, i, fa, pe, fi, g=g: (b, head(h, g, pe), 0, i))
               for g in range(hg)]
    k_specs = [pl.BlockSpec((1, 1, s, AUG),
                            lambda b, h, i, fa, pe, fi, g=g: (b, head(h, g, pe), 0, 0))
               for g in range(hg)]
    v_specs = [pl.BlockSpec((1, 1, s // tv, V_ROWS, tv),
                            lambda b, h, i, fa, pe, fi, g=g: (b, head(h, g, pe), 0, 0, 0))
               for g in range(hg)]
    return pl.pallas_call(
        functools.partial(_attn_kernel, tq=tq, tk=tk, hg=hg),
        grid_spec=pltpu.PrefetchScalarGridSpec(
            num_scalar_prefetch=3,
            grid=(bsz, nh // hg, s // tq),
            in_specs=q_specs + k_specs + v_specs,
            out_specs=pl.BlockSpec((1, hg * HEAD_DIM, tq), lambda b, h, i, fa, pe, fi: (b, h, i)),
            scratch_shapes=[pltpu.VMEM((hg, V_ROWS, tq), F32),
                            pltpu.VMEM((2, hg, tk, tq), BF16)],
        ),
        out_shape=jax.ShapeDtypeStruct((bsz, nh * HEAD_DIM, s), F32),
        compiler_params=pltpu.CompilerParams(
            dimension_semantics=("arbitrary", "arbitrary", "arbitrary"),
            vmem_limit_bytes=VMEM_LIMIT),
    )(fast, perm, first, *([qT] * hg), *([k] * hg), *([vT] * hg))


def _first_live_tiles(f2, perm, fast, *, tq, tk, hg):
    bsz, nh, s = f2.shape
    nq, nk = s // tq, s // tk
    f_q = f2[:, :, ::tq]
    f_k = f2[:, :, tk - 1::tk]
    dead = (f_q[:, :, :, None] - f_k[:, :, None, :]) < -SKIP_GAP
    before = (jnp.arange(nk)[None, :] + 1) * tk <= jnp.arange(nq)[:, None] * tq
    lead = jnp.min(jnp.where(dead & before, nk, jnp.arange(nk)), axis=-1)
    lead = jnp.min(lead[:, perm, :].reshape(bsz, nh // hg, hg, nq), axis=2)
    return jnp.where(fast, lead, 0).astype(jnp.int32).reshape(-1)


def _out_kernel(x_ref, aT_ref, un_ref, mod_ref, ba_ref, g2_ref, wo_ref, w1_ref, w2_ref, o_ref):
    a_w = aT_ref.shape[1]
    d_ff = w1_ref.shape[1]
    aT = aT_ref[0]
    inv = lax.rsqrt(jnp.mean(aT * aT, axis=0, keepdims=True) + EPS)
    an = ((aT * inv).T * ba_ref[...]).astype(BF16)
    o = jnp.dot(an, wo_ref[0:a_w, :], preferred_element_type=F32)
    o = o + jnp.dot(un_ref[0], wo_ref[a_w:, :], preferred_element_type=F32)
    x1 = x_ref[0] + mod_ref[0, 2:3, :] * o
    h = x1 * lax.rsqrt(jnp.mean(x1 * x1, axis=-1, keepdims=True) + EPS) * g2_ref[...]
    hb = (h * (1.0 + mod_ref[0, 4:5, :]) + mod_ref[0, 3:4, :]).astype(BF16)
    y = jnp.zeros_like(x1)
    for c in range(d_ff // FF_CHUNK):
        a = jnp.dot(hb, w1_ref[:, c * FF_CHUNK:(c + 1) * FF_CHUNK], preferred_element_type=F32)
        a = jnp.maximum(a, 0.0)
        y = y + jnp.dot((a * a).astype(BF16), w2_ref[c * FF_CHUNK:(c + 1) * FF_CHUNK, :],
                        preferred_element_type=F32)
    o_ref[0] = x1 + mod_ref[0, 5:6, :] * y


def _out(x, attnT, un, mod3, beta_a, norm_g, wo, w1, w2, *, tm):
    bsz, s, d = x.shape
    a_w = attnT.shape[1]
    c_w = un.shape[2]
    resident = lambda shape: pl.BlockSpec(shape, lambda b, t: (0,) * len(shape),
                                          pipeline_mode=pl.Buffered(1))
    return pl.pallas_call(
        _out_kernel,
        grid=(bsz, s // tm),
        in_specs=[
            pl.BlockSpec((1, tm, d), lambda b, t: (b, t, 0)),
            pl.BlockSpec((1, a_w, tm), lambda b, t: (b, 0, t)),
            pl.BlockSpec((1, tm, c_w), lambda b, t: (b, t, 0)),
            pl.BlockSpec((1, 6, d), lambda b, t: (b, 0, 0)),
            resident((1, a_w)),
            resident((1, d)),
            resident(wo.shape),
            resident(w1.shape),
            resident(w2.shape),
        ],
        out_specs=pl.BlockSpec((1, tm, d), lambda b, t: (b, t, 0)),
        out_shape=jax.ShapeDtypeStruct((bsz, s, d), F32),
        compiler_params=pltpu.CompilerParams(
            dimension_semantics=("arbitrary", "arbitrary"), vmem_limit_bytes=VMEM_LIMIT),
    )(x, attnT, un, mod3, beta_a, norm_g, wo, w1, w2)


def kernel(x, c, w_ada, b_ada, norm1_g, w_in, q_norm_g, k_norm_g, b_f, conv_w, conv_b,
           conv_ln_g, conv_ln_b, beta_attn, beta_conv, w_out, norm2_g, w_ff1, w_ff2):
    bsz, s, d = x.shape
    a_w = N_HEADS * HEAD_DIM
    c_w = conv_w.shape[2]
    assert s % TM_PROJ == 0 and s % TM_OUT == 0 and s % TK == 0 and s % TQ == 0
    assert (TK % TQ == 0 or TQ % TK == 0) and TM_PROJ % TV == 0 and TK % TV == 0
    assert w_in.shape[2] == 3 * a_w + N_HEADS + 2 * c_w
    for l in range(w_ada.shape[0]):
        mod3 = _ada(c, w_ada[l], b_ada[l]).reshape(bsz, 6, d)

        w = w_in[l]
        wf = jnp.zeros((d, BF16_ROWS), F32).at[:, :N_HEADS].set(w[:, 3 * a_w:3 * a_w + N_HEADS])
        wT = jnp.concatenate([w[:, :3 * a_w], wf], axis=1).T.astype(BF16)
        wc = w[:, 3 * a_w + N_HEADS:].astype(BF16)
        gain = q_norm_g[l] * k_norm_g[l] * (HEAD_DIM ** -0.5 * LOG2E)
        qg = jnp.broadcast_to(gain[:, None], (HEAD_DIM, LANES))
        bfb = jnp.broadcast_to(b_f[l][:, None], (N_HEADS, LANES))
        row = lambda v: v.reshape(1, -1)

        bound_coef = HEAD_DIM ** 0.5 * BOUND_SLACK
        worst_gap = 2.0 * bound_coef * HEAD_DIM ** 0.5 * jnp.max(jnp.abs(gain))
        fast = worst_gap <= -F32_MIN_EXP2 - 4.0
        mcoef = jnp.broadcast_to(jnp.where(fast, bound_coef, 0.0).astype(F32), (1, LANES))

        qT, k, vT, un, f2 = _proj(x, mod3, row(norm1_g[l]), wT, wc, qg, bfb, mcoef, conv_w[l],
                                  row(conv_b[l]), row(conv_ln_g[l]), row(conv_ln_b[l]),
                                  row(beta_conv[l]), tm=TM_PROJ, tv=TV)

        decay = jnp.mean(f2[:, :, -1], axis=0)
        ids = jnp.arange(N_HEADS)
        rank = jnp.sum((decay[None, :] < decay[:, None])
                       | ((decay[None, :] == decay[:, None]) & (ids[None, :] < ids[:, None])), axis=1)
        perm = jnp.sum(jnp.where(rank[None, :] == ids[:, None], ids[None, :], 0), axis=1).astype(jnp.int32)
        first = _first_live_tiles(f2, perm, fast, tq=TQ, tk=TK, hg=HEAD_GROUP)
        attnT = _attention(fast.astype(jnp.int32).reshape(1), perm, first, qT, k, vT,
                           tq=TQ, tk=TK, hg=HEAD_GROUP)
        by_head = lambda v: v.reshape((N_HEADS, HEAD_DIM) + v.shape[1:])[perm].reshape(v.shape)
        wo = jnp.concatenate([by_head(w_out[l][:a_w]), w_out[l][a_w:]], axis=0).astype(BF16)
        x = _out(x, attnT, un, mod3, row(by_head(beta_attn[l])), row(norm2_g[l]),
                 wo, w_ff1[l].astype(BF16), w_ff2[l].astype(BF16), tm=TM_OUT)
    return x
```
